```python
import math
import jax, jax.numpy as jnp
from jax import lax
import numpy as np

D_MODEL = 1024
BATCH = 4
SEQ = 8192
DEPTH = 1

GRID_W = 64
CTX_LEN = 256
N_HEADS = 8
N_KV_HEADS = 2
HEAD_DIM = 128
Q_GROUP = N_HEADS // N_KV_HEADS
ATTN_WIDTH = N_HEADS * HEAD_DIM
KV_WIDTH = N_KV_HEADS * HEAD_DIM
ROPE_AXIS_DIM = HEAD_DIM // 2
ROPE_THETA = 10000.0
Q_BLOCK = 128
D_HYENA = D_MODEL // 2
HYENA_ORDER = 2
FILTER_EMB_DIM = 33
FILTER_HIDDEN = 64
FILTER_INNER = 2
DECAY_TARGET = 1e-2
FAST_DECAY_PCT = 0.3
SLOW_DECAY_PCT = 1.5
N_BRANCHES = 2
D_FF = 2816
CONV_W = 3
RMS_EPS = 1e-6
PROJ_WIDTH = ATTN_WIDTH + 2 * KV_WIDTH + (HYENA_ORDER + 1) * D_HYENA + N_BRANCHES * D_MODEL
N_MOD = 6

kernel_name = 'hybrid_gqa_hyena_convffn_prefix_dit'


def rmsnorm(x, w):
    xf = x.astype(jnp.float32)
    y = xf * lax.rsqrt(jnp.mean(xf * xf, axis=-1, keepdims=True) + RMS_EPS)
    return (y * w.astype(jnp.float32)).astype(x.dtype)


def modulation(cvec, w_mod, b_mod):
    m = jax.nn.silu(cvec) @ w_mod + b_mod
    return jnp.split(m[:, None, :], N_MOD, axis=-1)


def modulate(x, w, shift, scale):
    return rmsnorm(x, w) * (1.0 + scale) + shift


def dwconv3(u, w, b):
    up = jnp.pad(u, ((0, 0), (1, 1), (0, 0)))
    return up[:, :-2] * w[0] + up[:, 1:-1] * w[1] + up[:, 2:] * w[2] + b


def axial_rope_tables(L):
    rows = L // GRID_W
    row = jnp.repeat(jnp.arange(rows), GRID_W).astype(jnp.float32)
    col = jnp.tile(jnp.arange(GRID_W), rows).astype(jnp.float32)
    freqs = ROPE_THETA ** (-jnp.arange(0, ROPE_AXIS_DIM, 2, dtype=jnp.float32) / ROPE_AXIS_DIM)
    ang = jnp.concatenate([row[:, None] * freqs, col[:, None] * freqs], axis=-1)
    return jnp.cos(ang), jnp.sin(ang)


def apply_rope(x, cos, sin):
    xr = x.reshape(x.shape[:-1] + (HEAD_DIM // 2, 2))
    x1, x2 = xr[..., 0], xr[..., 1]
    cs = cos[None, :, None, :].astype(x.dtype)
    sn = sin[None, :, None, :].astype(x.dtype)
    return jnp.stack([x1 * cs - x2 * sn, x1 * sn + x2 * cs], axis=-1).reshape(x.shape)


def split_proj(p):
    B, L = p.shape[:2]
    q, k, v, hy, gates = jnp.split(
        p, [ATTN_WIDTH, ATTN_WIDTH + KV_WIDTH, ATTN_WIDTH + 2 * KV_WIDTH,
            ATTN_WIDTH + 2 * KV_WIDTH + (HYENA_ORDER + 1) * D_HYENA], axis=-1)
    q = q.reshape(B, L, N_HEADS, HEAD_DIM)
    k = k.reshape(B, L, N_KV_HEADS, HEAD_DIM)
    v = v.reshape(B, L, N_KV_HEADS, HEAD_DIM)
    return q, k, v, hy, gates


def attend(q, k_all, v_all):
    B, Lq = q.shape[:2]
    nblk = Lq // Q_BLOCK
    qb = q.reshape(B, nblk, Q_BLOCK, N_KV_HEADS, Q_GROUP, HEAD_DIM).transpose(1, 0, 2, 3, 4, 5)
    scale = HEAD_DIM ** -0.5

    def one_block(qi):
        s = jnp.einsum('bqhgd,bkhd->bhgqk', qi, k_all, preferred_element_type=jnp.float32) * scale
        p = jax.nn.softmax(s, axis=-1).astype(v_all.dtype)
        return jnp.einsum('bhgqk,bkhd->bqhgd', p, v_all)

    o = lax.map(one_block, qb)
    return o.transpose(1, 0, 2, 3, 4, 5).reshape(B, Lq, ATTN_WIDTH)


def hyena_filters(L, w1, b1, w_inner, b_inner, freq, w_out):
    t_idx = jnp.arange(L, dtype=jnp.float32)
    t = t_idx / max(L - 1, 1)
    bands = (FILTER_EMB_DIM - 1) // 2
    f = jnp.linspace(1e-4, bands - 1, bands, dtype=jnp.float32)
    wpos = 2.0 * math.pi * t_idx / L
    z = jnp.concatenate([t[:, None], jnp.cos(wpos[:, None] * f), -jnp.sin(wpos[:, None] * f)], axis=-1)
    h = jnp.sin(freq * (z @ w1 + b1))
    for j in range(FILTER_INNER):
        h = jnp.sin(freq * (h @ w_inner[j] + b_inner[j]))
    h = h @ w_out
    min_decay = math.log(DECAY_TARGET) / SLOW_DECAY_PCT
    max_decay = math.log(DECAY_TARGET) / FAST_DECAY_PCT
    deltas = jnp.linspace(min_decay, max_decay, D_HYENA, dtype=jnp.float32)
    decay = jnp.exp(-t[:, None] * jnp.abs(deltas))
    h = (h.reshape(L, 2, HYENA_ORDER, D_HYENA) * decay[:, None, None, :]).astype(jnp.float32)
    h_fwd, h_bwd = h[:, 0], h[:, 1]
    k = jnp.concatenate([h_fwd[:1] + h_bwd[:1], h_fwd[1:],
                         jnp.zeros((1, HYENA_ORDER, D_HYENA), jnp.float32), h_bwd[:0:-1]], axis=0)
    return jnp.fft.rfft(k, axis=0)


def fft_long_conv(u, K):
    L = u.shape[1]
    U = jnp.fft.rfft(u.astype(jnp.float32), n=2 * L, axis=1)
    y = jnp.fft.irfft(U * K[None], n=2 * L, axis=1)[:, :L]
    return y.astype(u.dtype)


def hyena(streams, conv_w, conv_b, w1, b1, w_inner, b_inner, freq, w_out, skip):
    L = streams.shape[1]
    u = dwconv3(streams, conv_w, conv_b)
    x1, x2, v = jnp.split(u, HYENA_ORDER + 1, axis=-1)
    K = hyena_filters(L, w1, b1, w_inner, b_inner, freq, w_out)
    z = v
    for o, gate in enumerate((x1, x2)):
        z = gate * (fft_long_conv(z, K[:, o]) + skip[o] * z)
    return z


def merge(attn_o, hy_o, gates, w_attn_out, w_hy_out, w_o, b_o):
    g_attn, g_hy = jnp.split(gates, N_BRANCHES, axis=-1)
    mixed = jax.nn.sigmoid(g_attn) * (attn_o @ w_attn_out) + jax.nn.sigmoid(g_hy) * (hy_o @ w_hy_out)
    return mixed @ w_o + b_o


def conv_ffn(h, w_up, b_up, conv_w, conv_b, w_down, b_down):
    u = dwconv3(h @ w_up + b_up, conv_w, conv_b)
    a, g = jnp.split(u, 2, axis=-1)
    return (jax.nn.gelu(a) * g) @ w_down + b_down


def setup_inputs(seed: int = 0) -> dict:
    key = jax.random.key(seed)
    ks = jax.random.split(key, 32)

    def nrm(k, shape, scale):
        return jax.random.normal(k, shape, jnp.float32) * scale

    def gain(k, shape):
        return 1.0 + nrm(k, shape, 0.02)

    L = DEPTH
    return {
        'x': nrm(ks[0], (BATCH, SEQ, D_MODEL), 1.0),
        'c': nrm(ks[1], (BATCH, D_MODEL), 1.0),
        'ctx': nrm(ks[2], (BATCH, CTX_LEN, D_MODEL), 1.0),
        'c_ctx': nrm(ks[3], (D_MODEL,), 1.0),
        'w_mod': nrm(ks[4], (L, D_MODEL, N_MOD * D_MODEL), 0.5 * D_MODEL ** -0.5),
        'b_mod': nrm(ks[5], (L, N_MOD * D_MODEL), 0.02),
        'norm1_w': gain(ks[6], (L, D_MODEL)),
        'norm2_w': gain(ks[7], (L, D_MODEL)),
        'w_in': nrm(ks[8], (L, D_MODEL, PROJ_WIDTH), D_MODEL ** -0.5),
        'b_in': nrm(ks[9], (L, PROJ_WIDTH), 0.02),
        'q_norm_w': gain(ks[10], (L, HEAD_DIM)),
        'k_norm_w': gain(ks[11], (L, HEAD_DIM)),
        'hy_conv_w': nrm(ks[12], (L, CONV_W, (HYENA_ORDER + 1) * D_HYENA), CONV_W ** -0.5),
        'hy_conv_b': nrm(ks[13], (L, (HYENA_ORDER + 1) * D_HYENA), 0.02),
        'filt_w1': nrm(ks[14], (L, FILTER_EMB_DIM, FILTER_HIDDEN), FILTER_EMB_DIM ** -0.5),
        'filt_b1': nrm(ks[15], (L, FILTER_HIDDEN), 0.02),
        'filt_w_inner': nrm(ks[16], (L, FILTER_INNER, FILTER_HIDDEN, FILTER_HIDDEN), FILTER_HIDDEN ** -0.5),
        'filt_b_inner': nrm(ks[17], (L, FILTER_INNER, FILTER_HIDDEN), 0.02),
        'filt_freq': gain(ks[18], (L, FILTER_HIDDEN)),
        'filt_w_out': nrm(ks[19], (L, FILTER_HIDDEN, 2 * HYENA_ORDER * D_HYENA), 0.1 * FILTER_HIDDEN ** -0.5),
        'hy_skip': nrm(ks[20], (L, HYENA_ORDER, D_HYENA), 0.1),
        'w_attn_out': nrm(ks[21], (L, ATTN_WIDTH, D_MODEL), ATTN_WIDTH ** -0.5),
        'w_hy_out': nrm(ks[22], (L, D_HYENA, D_MODEL), D_HYENA ** -0.5),
        'w_o': nrm(ks[23], (L, D_MODEL, D_MODEL), D_MODEL ** -0.5),
        'b_o': nrm(ks[24], (L, D_MODEL), 0.02),
        'w_up': nrm(ks[25], (L, D_MODEL, 2 * D_FF), D_MODEL ** -0.5),
        'b_up': nrm(ks[26], (L, 2 * D_FF), 0.02),
        'ffn_conv_w': nrm(ks[27], (L, CONV_W, 2 * D_FF), CONV_W ** -0.5),
        'ffn_conv_b': nrm(ks[28], (L, 2 * D_FF), 0.02),
        'w_down': nrm(ks[29], (L, D_FF, D_MODEL), D_FF ** -0.5),
        'b_down': nrm(ks[30], (L, D_MODEL), 0.02),
        'final_norm_w': gain(ks[31], (D_MODEL,)),
    }


def reference(x, c, ctx, c_ctx, w_mod, b_mod, norm1_w, norm2_w, w_in, b_in, q_norm_w, k_norm_w,
              hy_conv_w, hy_conv_b, filt_w1, filt_b1, filt_w_inner, filt_b_inner, filt_freq,
              filt_w_out, hy_skip, w_attn_out, w_hy_out, w_o, b_o, w_up, b_up, ffn_conv_w,
              ffn_conv_b, w_down, b_down, final_norm_w):
    cos, sin = axial_rope_tables(x.shape[1])
    xc = ctx
    for i in range(DEPTH):
        update_ctx = i < DEPTH - 1
        sh1, sc1, g1, sh2, sc2, g2 = modulation(c, w_mod[i], b_mod[i])
        csh1, csc1, cg1, csh2, csc2, cg2 = modulation(c_ctx[None], w_mod[i], b_mod[i])

        def hyena_i(streams):
            return hyena(streams, hy_conv_w[i], hy_conv_b[i], filt_w1[i], filt_b1[i], filt_w_inner[i],
                         filt_b_inner[i], filt_freq[i], filt_w_out[i], hy_skip[i])

        def merge_i(attn_o, hy_o, gates):
            return merge(attn_o, hy_o, gates, w_attn_out[i], w_hy_out[i], w_o[i], b_o[i])

        def ffn_i(h):
            return conv_ffn(h, w_up[i], b_up[i], ffn_conv_w[i], ffn_conv_b[i], w_down[i], b_down[i])

        h = modulate(x, norm1_w[i], sh1, sc1)
        hc = modulate(xc, norm1_w[i], csh1, csc1)
        q, k, v, hy, gates = split_proj(h @ w_in[i] + b_in[i])
        qc, kc, vc, hyc, gatesc = split_proj(hc @ w_in[i] + b_in[i])
        q = apply_rope(rmsnorm(q, q_norm_w[i]), cos, sin)
        k = apply_rope(rmsnorm(k, k_norm_w[i]), cos, sin)
        kc = rmsnorm(kc, k_norm_w[i])
        k_all = jnp.concatenate([kc, k], axis=1)
        v_all = jnp.concatenate([vc, v], axis=1)
        attn_o = attend(q, k_all, v_all)
        hy_o = hyena_i(hy)
        x_new = x + g1 * merge_i(attn_o, hy_o, gates)
        x_new = x_new + g2 * ffn_i(modulate(x_new, norm2_w[i], sh2, sc2))

        if update_ctx:
            qc = rmsnorm(qc, q_norm_w[i])
            attn_c = attend(qc, kc, vc)
            hy_c = hyena_i(hyc)
            xc = xc + cg1 * merge_i(attn_c, hy_c, gatesc)
            xc = xc + cg2 * ffn_i(modulate(xc, norm2_w[i], csh2, csc2))
        x = x_new
    return rmsnorm(x, final_norm_w)
```

```python
import functools
import math

import numpy as np
import jax
import jax.numpy as jnp
from jax import lax
from jax.experimental import pallas as pl
from jax.experimental.pallas import tpu as pltpu

F32 = jnp.float32
BF16 = jnp.bfloat16
HIGHEST = lax.Precision.HIGHEST

GRID_W = 64
N_HEADS = 8
N_KV_HEADS = 2
Q_GROUP = N_HEADS // N_KV_HEADS
HEAD_DIM = 128
ROPE_THETA = 10000.0
RMS_EPS = 1e-6
HYENA_ORDER = 2
FILTER_EMB_DIM = 33
DECAY_TARGET = 1e-2
FAST_DECAY_PCT = 0.3
SLOW_DECAY_PCT = 1.5
N_MOD = 6

LANES = 128
SUBLANES = 8
BF16_ROWS = 16
VMEM_LIMIT = 56 * 1024 * 1024

SLAB = 128
T_PITCH = SLAB + SUBLANES
S_PITCH = 2 * SLAB + SUBLANES

T_TOK = 512
T_Q = 256
T_KV = 512
T_FILT = 1024
FF_CHUNK = 256
HALO_F32 = SUBLANES
HALO_BF16 = BF16_ROWS


def _round_up(n, m):
    return (n + m - 1) // m * m


def _rms(x):
    return x * lax.rsqrt(jnp.mean(x * x, axis=-1, keepdims=True) + RMS_EPS)


def _mod_norm(x, w, shift, scale):
    return (_rms(x) * w) * (1.0 + scale) + shift


def _const_spec(shape):
    n = len(shape)
    return pl.BlockSpec(shape, lambda *_: (0,) * n, pipeline_mode=pl.Buffered(1))


def _modulation_kernel(c_ref, w_ref, b_ref, o_ref):
    c = c_ref[...]
    s = c * jax.nn.sigmoid(c)
    o_ref[...] = jnp.dot(s, w_ref[...], precision=HIGHEST, preferred_element_type=F32) + b_ref[...]


def _modulation(cvec, w_mod, b_mod):
    rows, d = cvec.shape
    width = w_mod.shape[1]
    blk = d
    return pl.pallas_call(
        _modulation_kernel,
        out_shape=jax.ShapeDtypeStruct((rows, width), F32),
        grid=(width // blk,),
        in_specs=[pl.BlockSpec((rows, d), lambda j: (0, 0)),
                  pl.BlockSpec((d, blk), lambda j: (0, j)),
                  pl.BlockSpec((1, blk), lambda j: (0, j))],
        out_specs=pl.BlockSpec((rows, blk), lambda j: (0, j)),
        compiler_params=pltpu.CompilerParams(dimension_semantics=("arbitrary",)),
        name="modulation",
    )(cvec, w_mod, b_mod.reshape(1, width))


def _head_norm(p, w):
    return _rms(p) * w


def _rope(p, cos, sin):
    return p * cos + pltpu.roll(p, HEAD_DIM // 2, axis=1) * sin


def _inproj_kernel(xp_ref, xm_ref, xn_ref, mod_ref, n1_ref, w_ref, b_ref, qn_ref, kn_ref,
                   cos_ref, sin_ref, cw_ref, cb_ref,
                   q_ref, k_ref, v_ref, u_ref, pbuf, *, tok, d_hy3):
    i = pl.program_id(1)
    nt = pl.num_programs(1)
    shift = mod_ref[0, 0:1, :]
    scale = mod_ref[0, 1:2, :]
    n1 = n1_ref[...]
    hm = _mod_norm(xm_ref[0], n1, shift, scale).astype(BF16)
    halo = jnp.concatenate([xp_ref[0], xn_ref[0]], axis=0)
    hh = _mod_norm(halo, n1, shift, scale).astype(BF16)

    qw = N_HEADS * HEAD_DIM
    kw = N_KV_HEADS * HEAD_DIM
    hy0 = qw + 2 * kw
    p = jnp.dot(hm, w_ref[...], preferred_element_type=F32) + b_ref[...]
    cos = cos_ref[...]
    sin = sin_ref[...]
    qn = qn_ref[...]
    kn = kn_ref[...]
    for h in range(N_HEADS):
        ph = p[:, h * HEAD_DIM:(h + 1) * HEAD_DIM]
        q_ref[0, h] = _rope(_head_norm(ph, qn), cos, sin).astype(BF16)
    for h in range(N_KV_HEADS):
        ph = p[:, qw + h * HEAD_DIM: qw + (h + 1) * HEAD_DIM]
        k_ref[0, h] = _rope(_head_norm(ph, kn), cos, sin).astype(BF16)
        v_ref[0, h] = p[:, qw + kw + h * HEAD_DIM: qw + kw + (h + 1) * HEAD_DIM].astype(BF16)

    ph = jnp.dot(hh, w_ref[:, hy0:], preferred_element_type=F32) + b_ref[:, hy0:]
    top = jnp.where(i == 0, 0.0, ph[:HALO_F32])
    bot = jnp.where(i == nt - 1, 0.0, ph[HALO_F32:])
    pbuf[0:HALO_F32, :] = top
    pbuf[HALO_F32:HALO_F32 + tok, :] = p[:, hy0:]
    pbuf[HALO_F32 + tok:, :] = bot
    u = (cw_ref[0:1, :] * pbuf[HALO_F32 - 1:HALO_F32 - 1 + tok, :]
         + cw_ref[1:2, :] * pbuf[HALO_F32:HALO_F32 + tok, :]
         + cw_ref[2:3, :] * pbuf[HALO_F32 + 1:HALO_F32 + 1 + tok, :]
         + cb_ref[...])
    pad = jnp.zeros((T_PITCH - SLAB, d_hy3), F32)
    for j in range(tok // SLAB):
        u_ref[0, j * T_PITCH:j * T_PITCH + SLAB, :] = u[j * SLAB:(j + 1) * SLAB]
        u_ref[0, j * T_PITCH + SLAB:(j + 1) * T_PITCH, :] = pad


def _inproj(x, mod, n1w, w, b, qn, kn, cos_t, sin_t, cw, cb, ctx_len):
    bsz, seq, d = x.shape
    tok = T_TOK
    nt = seq // tok
    width = w.shape[1]
    qw = N_HEADS * HEAD_DIM
    kw = N_KV_HEADS * HEAD_DIM
    d_hy3 = width - qw - 2 * kw
    hb = tok // HALO_F32
    n_hblk = seq // HALO_F32
    kernel = functools.partial(_inproj_kernel, tok=tok, d_hy3=d_hy3)
    rows_p = (seq // SLAB) * T_PITCH
    out_shape = (
        jax.ShapeDtypeStruct((bsz, N_HEADS, seq, HEAD_DIM), BF16),
        jax.ShapeDtypeStruct((bsz, N_KV_HEADS, seq + ctx_len, HEAD_DIM), BF16),
        jax.ShapeDtypeStruct((bsz, N_KV_HEADS, seq + ctx_len, HEAD_DIM), BF16),
        jax.ShapeDtypeStruct((bsz, rows_p, d_hy3), F32),
    )
    return pl.pallas_call(
        kernel,
        out_shape=out_shape,
        grid=(bsz, nt),
        in_specs=[
            pl.BlockSpec((1, HALO_F32, d), lambda b_, i: (b_, jnp.maximum(i * hb - 1, 0), 0)),
            pl.BlockSpec((1, tok, d), lambda b_, i: (b_, i, 0)),
            pl.BlockSpec((1, HALO_F32, d), lambda b_, i: (b_, jnp.minimum((i + 1) * hb, n_hblk - 1), 0)),
            pl.BlockSpec((1, N_MOD, d), lambda b_, i: (b_, 0, 0)),
            _const_spec((1, d)),
            _const_spec((d, width)),
            _const_spec((1, width)),
            _const_spec((1, HEAD_DIM)),
            _const_spec((1, HEAD_DIM)),
            pl.BlockSpec((tok, HEAD_DIM), lambda b_, i: (i, 0)),
            pl.BlockSpec((tok, HEAD_DIM), lambda b_, i: (i, 0)),
            _const_spec((3, d_hy3)),
            _const_spec((1, d_hy3)),
        ],
        out_specs=(
            pl.BlockSpec((1, N_HEADS, tok, HEAD_DIM), lambda b_, i: (b_, 0, i, 0)),
            pl.BlockSpec((1, N_KV_HEADS, tok, HEAD_DIM), lambda b_, i: (b_, 0, i, 0)),
            pl.BlockSpec((1, N_KV_HEADS, tok, HEAD_DIM), lambda b_, i: (b_, 0, i, 0)),
            pl.BlockSpec((1, (tok // SLAB) * T_PITCH, d_hy3), lambda b_, i: (b_, i, 0)),
        ),
        scratch_shapes=[pltpu.VMEM((tok + 2 * HALO_F32, d_hy3), F32)],
        compiler_params=pltpu.CompilerParams(
            dimension_semantics=("arbitrary", "arbitrary"), vmem_limit_bytes=VMEM_LIMIT),
        name="inproj",
    )(x, x, x, mod, n1w, w, b, qn, kn, cos_t, sin_t, cw, cb)


def _ctx_kv_kernel(c_ref, mod_ref, n1_ref, w_ref, b_ref, kn_ref, kin_ref, vin_ref, k_ref, v_ref):
    del kin_ref, vin_ref
    h = _mod_norm(c_ref[0], n1_ref[...], mod_ref[0, 0:1, :], mod_ref[0, 1:2, :]).astype(BF16)
    p = jnp.dot(h, w_ref[...], preferred_element_type=F32) + b_ref[...]
    kw = N_KV_HEADS * HEAD_DIM
    kn = kn_ref[...]
    for hd in range(N_KV_HEADS):
        k_ref[0, hd] = _head_norm(p[:, hd * HEAD_DIM:(hd + 1) * HEAD_DIM], kn).astype(BF16)
        v_ref[0, hd] = p[:, kw + hd * HEAD_DIM: kw + (hd + 1) * HEAD_DIM].astype(BF16)


def _ctx_kv(ctx, mod, n1w, w, b, kn, k_all, v_all, seq):
    bsz, ctx_len, d = ctx.shape
    ctx_row = bsz
    blk = seq // ctx_len
    any_spec = pl.BlockSpec(memory_space=pl.ANY)
    return pl.pallas_call(
        _ctx_kv_kernel,
        out_shape=(jax.ShapeDtypeStruct(k_all.shape, k_all.dtype),
                   jax.ShapeDtypeStruct(v_all.shape, v_all.dtype)),
        grid=(bsz,),
        in_specs=[
            pl.BlockSpec((1, ctx_len, d), lambda b_: (b_, 0, 0)),
            pl.BlockSpec((1, N_MOD, d), lambda b_: (ctx_row, 0, 0)),
            _const_spec((1, d)),
            _const_spec(w.shape),
            _const_spec(b.shape),
            _const_spec((1, HEAD_DIM)),
            any_spec, any_spec,
        ],
        out_specs=(
            pl.BlockSpec((1, N_KV_HEADS, ctx_len, HEAD_DIM), lambda b_: (b_, 0, blk, 0)),
            pl.BlockSpec((1, N_KV_HEADS, ctx_len, HEAD_DIM), lambda b_: (b_, 0, blk, 0)),
        ),
        input_output_aliases={6: 0, 7: 1},
        compiler_params=pltpu.CompilerParams(dimension_semantics=("arbitrary",)),
        name="ctx_kv",
    )(ctx, mod, n1w, w, b, kn, k_all, v_all)


def _attn_kernel(q_ref, k_ref, v_ref, o_ref, *, n_full, tail):
    tq = q_ref.shape[2]
    rows = Q_GROUP * tq
    q = q_ref[0].reshape(rows, HEAD_DIM)

    def step(kc, vc, carry):
        m, l, acc = carry
        s = lax.dot_general(q, kc, (((1,), (1,)), ((), ())), preferred_element_type=F32)
        m_new = jnp.maximum(m, jnp.max(s, axis=-1, keepdims=True))
        alpha = jnp.exp(m - m_new)
        p = jnp.exp(s - m_new)
        l = alpha * l + jnp.sum(p, axis=-1, keepdims=True)
        acc = alpha * acc + jnp.dot(p.astype(BF16), vc, preferred_element_type=F32)
        return m_new, l, acc

    def body(c, carry):
        start = pl.multiple_of(c * T_KV, T_KV)
        return step(k_ref[0, 0, pl.ds(start, T_KV), :], v_ref[0, 0, pl.ds(start, T_KV), :], carry)

    init = (jnp.full((rows, 1), -jnp.inf, F32), jnp.zeros((rows, 1), F32),
            jnp.zeros((rows, HEAD_DIM), F32))
    carry = lax.fori_loop(0, n_full, body, init)
    if tail:
        t0 = n_full * T_KV
        carry = step(k_ref[0, 0, t0:t0 + tail, :], v_ref[0, 0, t0:t0 + tail, :], carry)
    _, l, acc = carry
    out = acc / l
    for g in range(Q_GROUP):
        o_ref[0, :, g * HEAD_DIM:(g + 1) * HEAD_DIM] = out[g * tq:(g + 1) * tq].astype(o_ref.dtype)


def _attention(q, k_all, v_all):
    bsz, _, seq, _ = q.shape
    n_keys = k_all.shape[2]
    tq = T_Q
    kernel = functools.partial(_attn_kernel, n_full=n_keys // T_KV, tail=n_keys % T_KV)
    return pl.pallas_call(
        kernel,
        out_shape=jax.ShapeDtypeStruct((bsz, seq, N_HEADS * HEAD_DIM), BF16),
        grid=(bsz, N_KV_HEADS, seq // tq),
        in_specs=[
            pl.BlockSpec((1, Q_GROUP, tq, HEAD_DIM), lambda b_, h, i: (b_, h, i, 0)),
            pl.BlockSpec((1, 1, n_keys, HEAD_DIM), lambda b_, h, i: (b_, h, 0, 0)),
            pl.BlockSpec((1, 1, n_keys, HEAD_DIM), lambda b_, h, i: (b_, h, 0, 0)),
        ],
        out_specs=pl.BlockSpec((1, tq, Q_GROUP * HEAD_DIM), lambda b_, h, i: (b_, i, h)),
        compiler_params=pltpu.CompilerParams(
            dimension_semantics=("arbitrary", "arbitrary", "arbitrary"), vmem_limit_bytes=VMEM_LIMIT),
        name="attention",
    )(q, k_all, v_all)


def _filter_kernel(z_ref, tab_ref, dl_ref, w1_ref, b1_ref, wi_ref, bi_ref, fr_ref, wf_ref, wb_ref,
                   g_ref, *, rows, d_hy, n_inner):
    fr = fr_ref[...]
    h = jnp.sin(fr * (jnp.dot(z_ref[...], w1_ref[...], precision=HIGHEST,
                              preferred_element_type=F32) + b1_ref[...]))
    for j in range(n_inner):
        h = jnp.sin(fr * (jnp.dot(h, wi_ref[j], precision=HIGHEST,
                                  preferred_element_type=F32) + bi_ref[j]))
    pf = jnp.dot(h, wf_ref[...], precision=HIGHEST, preferred_element_type=F32)
    pb = jnp.dot(h, wb_ref[...], precision=HIGHEST, preferred_element_type=F32)
    t = tab_ref[:, 0:1]
    mf = tab_ref[:, 1:2]
    mb = tab_ref[:, 2:3]
    decay = jnp.exp(-t * dl_ref[...])
    pad = jnp.zeros((T_PITCH - SLAB, d_hy), F32)
    for o in range(HYENA_ORDER):
        g = (mf * pf[:, o * d_hy:(o + 1) * d_hy] + mb * pb[:, o * d_hy:(o + 1) * d_hy]) * decay
        for j in range(rows // SLAB):
            g_ref[o, j * T_PITCH:j * T_PITCH + SLAB, :] = g[j * SLAB:(j + 1) * SLAB]
            g_ref[o, j * T_PITCH + SLAB:(j + 1) * T_PITCH, :] = pad


def _filter_time(z2, tab, absdelta, w1p, b1, wi, bi, freq, wf, wb):
    n_rows = z2.shape[0]
    d_hy = absdelta.shape[1]
    rows = T_FILT
    n_inner = wi.shape[0]
    kernel = functools.partial(_filter_kernel, rows=rows, d_hy=d_hy, n_inner=n_inner)
    return pl.pallas_call(
        kernel,
        out_shape=jax.ShapeDtypeStruct((HYENA_ORDER, (n_rows // SLAB) * T_PITCH, d_hy), F32),
        grid=(n_rows // rows,),
        in_specs=[
            pl.BlockSpec((rows, z2.shape[1]), lambda i: (i, 0)),
            pl.BlockSpec((rows, tab.shape[1]), lambda i: (i, 0)),
            _const_spec(absdelta.shape), _const_spec(w1p.shape), _const_spec(b1.shape),
            _const_spec(wi.shape), _const_spec(bi.shape), _const_spec(freq.shape),
            _const_spec(wf.shape), _const_spec(wb.shape),
        ],
        out_specs=pl.BlockSpec((HYENA_ORDER, (rows // SLAB) * T_PITCH, d_hy), lambda i: (0, i, 0)),
        compiler_params=pltpu.CompilerParams(
            dimension_semantics=("arbitrary",), vmem_limit_bytes=VMEM_LIMIT),
        name="hyena_filter",
    )(z2, tab, absdelta, w1p, b1, wi, bi, freq, wf, wb)


def _forward_cross_slab(src_ref, n_in, ff_ref, s_ref, half):
    def body(r, carry):
        g = src_ref[pl.ds(r, n_in, stride=T_PITCH), :]
        a = jnp.dot(ff_ref[r], g.astype(BF16), preferred_element_type=F32)
        s_ref[pl.ds(r, half, stride=S_PITCH), :] = a[:half]
        s_ref[pl.ds(SLAB + r, half, stride=S_PITCH), :] = a[half:]
        return carry

    lax.fori_loop(0, SLAB, body, 0)


def _filter_spec_kernel(g_ref, ff_ref, fa_ref, k_ref, s_ref, *, n_in, n_p, half):
    _forward_cross_slab(g_ref.at[0], n_in, ff_ref, s_ref, half)
    fa = fa_ref[...]

    def slab(p, carry):
        start = pl.multiple_of(p * S_PITCH, SUBLANES)
        a = s_ref[pl.ds(start, 2 * SLAB), :]
        k_ref[0, p] = jnp.dot(fa, a.astype(BF16), preferred_element_type=F32).astype(k_ref.dtype)
        return carry

    lax.fori_loop(0, n_p, slab, 0)


def _filter_spectrum(g, ff_full, fa, n_p):
    n_ord, rows_p, d_hy = g.shape
    n_in = rows_p // T_PITCH
    half = ff_full.shape[1] // 2
    kernel = functools.partial(_filter_spec_kernel, n_in=n_in, n_p=n_p, half=half)
    return pl.pallas_call(
        kernel,
        out_shape=jax.ShapeDtypeStruct((n_ord, n_p, 2 * SLAB, d_hy), BF16),
        grid=(n_ord, d_hy // LANES),
        in_specs=[
            pl.BlockSpec((1, rows_p, LANES), lambda o, c: (o, 0, c)),
            _const_spec(ff_full.shape),
            _const_spec(fa.shape),
        ],
        out_specs=pl.BlockSpec((1, n_p, 2 * SLAB, LANES), lambda o, c: (o, 0, 0, c)),
        scratch_shapes=[pltpu.VMEM((half * S_PITCH, LANES), F32)],
        compiler_params=pltpu.CompilerParams(
            dimension_semantics=("arbitrary", "arbitrary"), vmem_limit_bytes=VMEM_LIMIT),
        name="hyena_filter_spectrum",
    )(g, ff_full, fa)


def _conv_kernel(z_ref, x_ref, k_ref, sk_ref, ff_ref, fa_ref, fai_ref, gi_ref, o_ref, s_ref,
                 *, n_a, n_p, half):
    _forward_cross_slab(z_ref.at[0], n_a, ff_ref, s_ref, half)
    fa = fa_ref[...]
    fai = fai_ref[...]

    def slab(p, carry):
        start = pl.multiple_of(p * S_PITCH, SUBLANES)
        a = s_ref[pl.ds(start, 2 * SLAB), :]
        x = jnp.dot(fa, a.astype(BF16), preferred_element_type=F32)
        kk = k_ref[0, p].astype(F32)
        xr, xi = x[:SLAB], x[SLAB:]
        kr, ki = kk[:SLAB], kk[SLAB:]
        y = jnp.concatenate([xr * kr - xi * ki, xr * ki + xi * kr], axis=0)
        s_ref[pl.ds(start, 2 * SLAB), :] = jnp.dot(fai, y.astype(BF16), preferred_element_type=F32)
        return carry

    lax.fori_loop(0, n_p, slab, 0)
    skip = sk_ref[0]

    def inverse(r, carry):
        br = s_ref[pl.ds(r, half, stride=S_PITCH), :]
        bi = s_ref[pl.ds(SLAB + r, half, stride=S_PITCH), :]
        bb = jnp.concatenate([br, bi], axis=0).astype(BF16)
        y = jnp.dot(gi_ref[r], bb, preferred_element_type=F32)
        z = z_ref[0, pl.ds(r, n_a, stride=T_PITCH), :]
        gate = x_ref[0, pl.ds(r, n_a, stride=T_PITCH), :]
        o_ref[0, pl.ds(r, n_a, stride=T_PITCH), :] = gate * (y + skip * z)
        return carry

    lax.fori_loop(0, SLAB, inverse, 0)
    pad = jnp.zeros((T_PITCH - SLAB, LANES), F32)
    for a in range(n_a):
        o_ref[0, a * T_PITCH + SLAB:(a + 1) * T_PITCH, :] = pad


def _hyena_conv(zin, z_blk0, gate, g_blk0, kspec, order, skip, ff, fa, fai, gi, d_hy):
    bsz, rows_p, _ = zin.shape
    n_a = rows_p // T_PITCH
    n_p = kspec.shape[1]
    half = ff.shape[1] // 2
    kernel = functools.partial(_conv_kernel, n_a=n_a, n_p=n_p, half=half)
    return pl.pallas_call(
        kernel,
        out_shape=jax.ShapeDtypeStruct((bsz, rows_p, d_hy), F32),
        grid=(d_hy // LANES, bsz),
        in_specs=[
            pl.BlockSpec((1, rows_p, LANES), lambda c, b_: (b_, 0, z_blk0 + c)),
            pl.BlockSpec((1, rows_p, LANES), lambda c, b_: (b_, 0, g_blk0 + c),
                         pipeline_mode=pl.Buffered(1)),
            pl.BlockSpec((1, n_p, 2 * SLAB, LANES), lambda c, b_: (order, 0, 0, c),
                         pipeline_mode=pl.Buffered(1)),
            pl.BlockSpec((1, 1, LANES), lambda c, b_: (order, 0, c)),
            _const_spec(ff.shape), _const_spec(fa.shape), _const_spec(fai.shape),
            _const_spec(gi.shape),
        ],
        out_specs=pl.BlockSpec((1, rows_p, LANES), lambda c, b_: (b_, 0, c)),
        scratch_shapes=[pltpu.VMEM((half * S_PITCH, LANES), F32)],
        compiler_params=pltpu.CompilerParams(
            dimension_semantics=("arbitrary", "arbitrary"), vmem_limit_bytes=VMEM_LIMIT),
        name=f"hyena_conv{order}",
    )(zin, gate, kspec, skip, ff, fa, fai, gi)


def _merge_kernel(x_ref, mod_ref, n1_ref, n2_ref, wg_ref, bg_ref, a_ref, hy_ref, wa_ref, wh_ref,
                  wo_ref, bo_ref, xo_ref, h2_ref, *, tok, d):
    x = x_ref[0]
    sh1, sc1, g1 = mod_ref[0, 0:1, :], mod_ref[0, 1:2, :], mod_ref[0, 2:3, :]
    sh2, sc2 = mod_ref[0, 3:4, :], mod_ref[0, 4:5, :]
    h1 = _mod_norm(x, n1_ref[...], sh1, sc1).astype(BF16)
    gates = jnp.dot(h1, wg_ref[...], preferred_element_type=F32) + bg_ref[...]
    hy = jnp.concatenate([hy_ref[0, j * T_PITCH:j * T_PITCH + SLAB, :] for j in range(tok // SLAB)],
                         axis=0)
    pa = jnp.dot(a_ref[0], wa_ref[...], preferred_element_type=F32)
    ph = jnp.dot(hy.astype(BF16), wh_ref[...], preferred_element_type=F32)
    mixed = jax.nn.sigmoid(gates[:, :d]) * pa + jax.nn.sigmoid(gates[:, d:]) * ph
    y = jnp.dot(mixed.astype(BF16), wo_ref[...], preferred_element_type=F32) + bo_ref[...]
    xn = x + g1 * y
    xo_ref[0] = xn
    h2_ref[0] = _mod_norm(xn, n2_ref[...], sh2, sc2).astype(BF16)


def _merge(x, mod, n1w, n2w, wg, bg, attn_o, hy_o, wa, wh, wo, bo):
    bsz, seq, d = x.shape
    tok = T_TOK
    d_hy = hy_o.shape[2]
    kernel = functools.partial(_merge_kernel, tok=tok, d=d)
    return pl.pallas_call(
        kernel,
        out_shape=(jax.ShapeDtypeStruct((bsz, seq, d), F32), jax.ShapeDtypeStruct((bsz, seq, d), BF16)),
        grid=(bsz, seq // tok),
        in_specs=[
            pl.BlockSpec((1, tok, d), lambda b_, i: (b_, i, 0)),
            pl.BlockSpec((1, N_MOD, d), lambda b_, i: (b_, 0, 0)),
            _const_spec((1, d)), _const_spec((1, d)),
            _const_spec(wg.shape), _const_spec(bg.shape),
            pl.BlockSpec((1, tok, attn_o.shape[2]), lambda b_, i: (b_, i, 0)),
            pl.BlockSpec((1, (tok // SLAB) * T_PITCH, d_hy), lambda b_, i: (b_, i, 0)),
            _const_spec(wa.shape), _const_spec(wh.shape), _const_spec(wo.shape), _const_spec(bo.shape),
        ],
        out_specs=(pl.BlockSpec((1, tok, d), lambda b_, i: (b_, i, 0)),
                   pl.BlockSpec((1, tok, d), lambda b_, i: (b_, i, 0))),
        compiler_params=pltpu.CompilerParams(
            dimension_semantics=("arbitrary", "arbitrary"), vmem_limit_bytes=VMEM_LIMIT),
        name="merge",
    )(x, mod, n1w, n2w, wg, bg, attn_o, hy_o, wa, wh, wo, bo)


def _gelu_tanh(x):
    return 0.5 * x * (1.0 + jnp.tanh(math.sqrt(2.0 / math.pi) * (x + 0.044715 * (x * x * x))))


def _ffn_kernel(hp_ref, hm_ref, hn_ref, x_ref, mod_ref, wu_ref, bu_ref, cw_ref, cb_ref, wd_ref, bd_ref,
                fn_ref, o_ref, pa_buf, pg_buf, *, tok, d_ff):
    i = pl.program_id(1)
    nt = pl.num_programs(1)
    hext = jnp.concatenate([hp_ref[0], hm_ref[0], hn_ref[0]], axis=0)
    rows = tok + 2 * HALO_BF16
    row_id = lax.broadcasted_iota(jnp.int32, (rows, 1), 0)
    valid = jnp.logical_and(jnp.logical_or(i > 0, row_id >= HALO_BF16),
                            jnp.logical_or(i < nt - 1, row_id < HALO_BF16 + tok))
    acc = jnp.zeros((tok, o_ref.shape[2]), F32)
    lo = HALO_BF16

    def conv(buf, col0):
        return (cw_ref[0:1, col0:col0 + FF_CHUNK] * buf[lo - 1:lo - 1 + tok, :]
                + cw_ref[1:2, col0:col0 + FF_CHUNK] * buf[lo:lo + tok, :]
                + cw_ref[2:3, col0:col0 + FF_CHUNK] * buf[lo + 1:lo + 1 + tok, :]
                + cb_ref[:, col0:col0 + FF_CHUNK])

    for c in range(d_ff // FF_CHUNK):
        ca = c * FF_CHUNK
        cg = d_ff + c * FF_CHUNK
        pa = jnp.dot(hext, wu_ref[:, ca:ca + FF_CHUNK], preferred_element_type=F32) + bu_ref[:, ca:ca + FF_CHUNK]
        pg = jnp.dot(hext, wu_ref[:, cg:cg + FF_CHUNK], preferred_element_type=F32) + bu_ref[:, cg:cg + FF_CHUNK]
        pa_buf[...] = jnp.where(valid, pa, 0.0)
        pg_buf[...] = jnp.where(valid, pg, 0.0)
        act = _gelu_tanh(conv(pa_buf, ca)) * conv(pg_buf, cg)
        acc = acc + jnp.dot(act.astype(BF16), wd_ref[ca:ca + FF_CHUNK, :], preferred_element_type=F32)
    g2 = mod_ref[0, 5:6, :]
    xn = x_ref[0] + g2 * (acc + bd_ref[...])
    o_ref[0] = _rms(xn) * fn_ref[...]


def _ffn(h2, x_new, mod, wu, bu, cw, cb, wd, bd, fnw):
    bsz, seq, d = x_new.shape
    tok = T_TOK
    d_ff = wd.shape[0]
    hb = tok // HALO_BF16
    n_hblk = seq // HALO_BF16
    rows = tok + 2 * HALO_BF16
    kernel = functools.partial(_ffn_kernel, tok=tok, d_ff=d_ff)
    return pl.pallas_call(
        kernel,
        out_shape=jax.ShapeDtypeStruct((bsz, seq, d), F32),
        grid=(bsz, seq // tok),
        in_specs=[
            pl.BlockSpec((1, HALO_BF16, d), lambda b_, i: (b_, jnp.maximum(i * hb - 1, 0), 0)),
            pl.BlockSpec((1, tok, d), lambda b_, i: (b_, i, 0)),
            pl.BlockSpec((1, HALO_BF16, d), lambda b_, i: (b_, jnp.minimum((i + 1) * hb, n_hblk - 1), 0)),
            pl.BlockSpec((1, tok, d), lambda b_, i: (b_, i, 0)),
            pl.BlockSpec((1, N_MOD, d), lambda b_, i: (b_, 0, 0)),
            _const_spec(wu.shape), _const_spec(bu.shape), _const_spec(cw.shape), _const_spec(cb.shape),
            _const_spec(wd.shape), _const_spec(bd.shape), _const_spec(fnw.shape),
        ],
        out_specs=pl.BlockSpec((1, tok, d), lambda b_, i: (b_, i, 0)),
        scratch_shapes=[pltpu.VMEM((rows, FF_CHUNK), F32), pltpu.VMEM((rows, FF_CHUNK), F32)],
        compiler_params=pltpu.CompilerParams(
            dimension_semantics=("arbitrary", "arbitrary"), vmem_limit_bytes=VMEM_LIMIT),
        name="conv_ffn",
    )(h2, h2, h2, x_new, mod, wu, bu, cw, cb, wd, bd, fnw)


def _rope_tables(seq):
    half = HEAD_DIM // 2
    t = np.arange(seq)
    row = (t // GRID_W).astype(np.float64)
    col = (t % GRID_W).astype(np.float64)
    freqs = ROPE_THETA ** (-np.arange(0, half, 2, dtype=np.float64) / half)
    ang = np.concatenate([row[:, None] * freqs, col[:, None] * freqs], axis=-1)
    cos = np.cos(ang)
    sin = np.sin(ang)
    cos_t = np.concatenate([cos, cos], axis=-1)
    sin_t = np.concatenate([-sin, sin], axis=-1)
    return jnp.asarray(cos_t, F32), jnp.asarray(sin_t, F32)


def _filter_tables(seq, d_hy):
    n = 2 * seq
    j = np.arange(n)
    lag = np.where(j < seq, j, n - j)
    t = lag / max(seq - 1, 1)
    bands = (FILTER_EMB_DIM - 1) // 2
    f = np.linspace(1e-4, bands - 1, bands)
    wpos = 2.0 * math.pi * lag / seq
    z = np.concatenate([t[:, None], np.cos(wpos[:, None] * f), -np.sin(wpos[:, None] * f)], axis=-1)
    z2 = np.zeros((n, LANES), np.float64)
    z2[:, :FILTER_EMB_DIM] = z
    tab = np.zeros((n, SUBLANES), np.float64)
    tab[:, 0] = t
    tab[:, 1] = (j < seq)
    tab[:, 2] = np.logical_or(j == 0, j > seq)
    min_decay = math.log(DECAY_TARGET) / SLOW_DECAY_PCT
    max_decay = math.log(DECAY_TARGET) / FAST_DECAY_PCT
    absdelta = np.abs(np.linspace(min_decay, max_decay, d_hy))[None, :]
    return jnp.asarray(z2, F32), jnp.asarray(tab, F32), jnp.asarray(absdelta, F32)


def _dft_tables(seq):
    n_a = seq // SLAB
    n_s = 2 * n_a
    n = 2 * seq
    n_p = n_a + 1
    half = _round_up(n_p, SUBLANES)
    p = np.arange(half)
    live = (p < n_p).astype(np.float64)
    r = np.arange(SLAB)
    a = np.arange(n_s)
    theta = 2.0 * math.pi * p[None, :, None] * (SLAB * a[None, None, :] + r[:, None, None]) / n
    ff = np.concatenate([np.cos(theta) * live[None, :, None], -np.sin(theta) * live[None, :, None]], axis=1)
    phi = 2.0 * math.pi * np.outer(r, r) / SLAB
    c, s = np.cos(phi), np.sin(phi)
    fa = np.block([[c, s], [-s, c]])
    fai = np.block([[c, -s], [s, c]])
    wgt = np.where((p == 0) | (p == n_a), 1.0, 2.0) * live / n
    th_i = 2.0 * math.pi * (SLAB * a[None, :n_a, None] + r[:, None, None]) * p[None, None, :] / n
    gi = np.concatenate([np.cos(th_i) * wgt[None, None, :], -np.sin(th_i) * wgt[None, None, :]], axis=2)
    to_bf = lambda m: jnp.asarray(m, F32).astype(BF16)
    return to_bf(ff), to_bf(ff[:, :, :n_a]), to_bf(fa), to_bf(fai), to_bf(gi), n_p


def kernel(x, c, ctx, c_ctx, w_mod, b_mod, norm1_w, norm2_w, w_in, b_in, q_norm_w, k_norm_w, hy_conv_w, hy_conv_b, filt_w1, filt_b1, filt_w_inner, filt_b_inner, filt_freq, filt_w_out, hy_skip, w_attn_out, w_hy_out, w_o, b_o, w_up, b_up, ffn_conv_w, ffn_conv_b, w_down, b_down, final_norm_w):
    bsz, seq, d = x.shape
    ctx_len = ctx.shape[1]
    d_hy = hy_skip.shape[2]
    assert w_mod.shape[0] == 1, "single-layer kernel"
    assert seq % T_TOK == 0 and seq % T_Q == 0 and seq % ctx_len == 0 and (2 * seq) % T_FILT == 0
    assert d_hy % LANES == 0 and w_down.shape[1] % FF_CHUNK == 0 and bsz + 1 <= SUBLANES

    cvec = jnp.zeros((SUBLANES, d), F32).at[:bsz].set(c).at[bsz].set(c_ctx)
    mod = _modulation(cvec, w_mod[0], b_mod[0]).reshape(SUBLANES, N_MOD, d)

    qw = N_HEADS * HEAD_DIM
    kw = N_KV_HEADS * HEAD_DIM
    hy0 = qw + 2 * kw
    g0 = hy0 + (HYENA_ORDER + 1) * d_hy
    perm = np.concatenate([np.arange(0, HEAD_DIM, 2), np.arange(1, HEAD_DIM, 2)])
    perm_q = (np.arange(N_HEADS)[:, None] * HEAD_DIM + perm[None, :]).reshape(-1)
    perm_k = qw + (np.arange(N_KV_HEADS)[:, None] * HEAD_DIM + perm[None, :]).reshape(-1)
    cols = np.concatenate([perm_q, perm_k, np.arange(qw + kw, g0)])
    w_main = w_in[0][:, cols].astype(BF16)
    b_main = b_in[0][cols].reshape(1, -1)
    w_kv = w_main[:, qw:hy0]
    b_kv = b_main[:, qw:hy0]
    w_gates = w_in[0][:, g0:].astype(BF16)
    b_gates = b_in[0][g0:].reshape(1, -1)
    qn = (q_norm_w[0][perm] * (HEAD_DIM ** -0.5)).reshape(1, HEAD_DIM)
    kn = k_norm_w[0][perm].reshape(1, HEAD_DIM)
    cos_t, sin_t = _rope_tables(seq)
    n1w = norm1_w[0].reshape(1, d)
    n2w = norm2_w[0].reshape(1, d)

    q, k_all, v_all, u = _inproj(x, mod, n1w, w_main, b_main, qn, kn, cos_t, sin_t,
                                 hy_conv_w[0], hy_conv_b[0].reshape(1, -1), ctx_len)
    k_all, v_all = _ctx_kv(ctx, mod, n1w, w_kv, b_kv, kn, k_all, v_all, seq)
    attn_o = _attention(q, k_all, v_all)

    z2, tab, absdelta = _filter_tables(seq, d_hy)
    w1p = jnp.zeros((LANES, filt_w1.shape[2]), F32).at[:FILTER_EMB_DIM].set(filt_w1[0])
    w_out = filt_w_out[0].reshape(-1, 2, HYENA_ORDER * d_hy)
    g_time = _filter_time(z2, tab, absdelta, w1p, filt_b1[0].reshape(1, -1), filt_w_inner[0],
                          filt_b_inner[0][:, None, :], filt_freq[0].reshape(1, -1),
                          w_out[:, 0], w_out[:, 1])
    ff_full, ff_half, fa, fai, gi, n_p = _dft_tables(seq)
    kspec = _filter_spectrum(g_time, ff_full, fa, n_p)
    skip = hy_skip[0].reshape(HYENA_ORDER, 1, d_hy)
    lanes_per_stream = d_hy // LANES
    z1 = _hyena_conv(u, 2 * lanes_per_stream, u, 0, kspec, 0, skip, ff_half, fa, fai, gi, d_hy)
    hy_o = _hyena_conv(z1, 0, u, lanes_per_stream, kspec, 1, skip, ff_half, fa, fai, gi, d_hy)

    x_new, h2 = _merge(x, mod, n1w, n2w, w_gates, b_gates, attn_o, hy_o,
                       w_attn_out[0].astype(BF16), w_hy_out[0].astype(BF16), w_o[0].astype(BF16),
                       b_o[0].reshape(1, d))
    return _ffn(h2, x_new, mod, w_up[0].astype(BF16), b_up[0].reshape(1, -1), ffn_conv_w[0],
                ffn_conv_b[0].reshape(1, -1), w_down[0].astype(BF16), b_down[0].reshape(1, d),
                final_norm_w.reshape(1, d))
```

```python
import functools
import math

import numpy as np
import jax
import jax.numpy as jnp
from jax import lax
from jax.experimental import pallas as pl
from jax.experimental.pallas import tpu as pltpu

F32 = jnp.float32
BF16 = jnp.bfloat16
HIGHEST = lax.Precision.HIGHEST

GRID_W = 64
N_HEADS = 8
N_KV_HEADS = 2
Q_GROUP = N_HEADS // N_KV_HEADS
HEAD_DIM = 128
ROPE_THETA = 10000.0
RMS_EPS = 1e-6
HYENA_ORDER = 2
FILTER_EMB_DIM = 33
DECAY_TARGET = 1e-2
FAST_DECAY_PCT = 0.3
SLOW_DECAY_PCT = 1.5
N_MOD = 6

LANES = 128
SUBLANES = 8
BF16_ROWS = 16
VMEM_LIMIT = 56 * 1024 * 1024

SLAB = 128
T_PITCH = SLAB + SUBLANES
S_PITCH = 2 * SLAB + SUBLANES

T_TOK = 512
T_Q = 256
KV_SUB = 256
CROSS_UNROLL = 4
SLAB_UNROLL = 2
T_FILT = 1024
FF_CHUNK = 256
HALO_F32 = SUBLANES
HALO_BF16 = BF16_ROWS


def _round_up(n, m):
    return (n + m - 1) // m * m


def _rms(x):
    return x * lax.rsqrt(jnp.mean(x * x, axis=-1, keepdims=True) + RMS_EPS)


def _mod_norm(x, w, shift, scale):
    return (_rms(x) * w) * (1.0 + scale) + shift


def _const_spec(shape):
    n = len(shape)
    return pl.BlockSpec(shape, lambda *_: (0,) * n, pipeline_mode=pl.Buffered(1))


def _modulation_kernel(c_ref, w_ref, b_ref, o_ref):
    c = c_ref[...]
    s = c * jax.nn.sigmoid(c)
    o_ref[...] = jnp.dot(s, w_ref[...], precision=HIGHEST, preferred_element_type=F32) + b_ref[...]


def _modulation(cvec, w_mod, b_mod):
    rows, d = cvec.shape
    width = w_mod.shape[1]
    blk = d
    return pl.pallas_call(
        _modulation_kernel,
        out_shape=jax.ShapeDtypeStruct((rows, width), F32),
        grid=(width // blk,),
        in_specs=[pl.BlockSpec((rows, d), lambda j: (0, 0)),
                  pl.BlockSpec((d, blk), lambda j: (0, j)),
                  pl.BlockSpec((1, blk), lambda j: (0, j))],
        out_specs=pl.BlockSpec((rows, blk), lambda j: (0, j)),
        compiler_params=pltpu.CompilerParams(dimension_semantics=("arbitrary",)),
        name="modulation",
    )(cvec, w_mod, b_mod.reshape(1, width))


def _head_norm(p, w):
    return _rms(p) * w


def _rope(p, cos, sin):
    return p * cos + pltpu.roll(p, HEAD_DIM // 2, axis=1) * sin


def _inproj_kernel(xp_ref, xm_ref, xn_ref, mod_ref, n1_ref, w_ref, b_ref, qn_ref, kn_ref,
                   cos_ref, sin_ref, cw_ref, cb_ref,
                   q_ref, k_ref, v_ref, u_ref, pbuf, *, tok, d_hy3):
    i = pl.program_id(1)
    nt = pl.num_programs(1)
    shift = mod_ref[0, 0:1, :]
    scale = mod_ref[0, 1:2, :]
    n1 = n1_ref[...]
    hm = _mod_norm(xm_ref[0], n1, shift, scale).astype(BF16)
    halo = jnp.concatenate([xp_ref[0], xn_ref[0]], axis=0)
    hh = _mod_norm(halo, n1, shift, scale).astype(BF16)

    qw = N_HEADS * HEAD_DIM
    kw = N_KV_HEADS * HEAD_DIM
    hy0 = qw + 2 * kw
    p = jnp.dot(hm, w_ref[...], preferred_element_type=F32) + b_ref[...]
    cos = cos_ref[...]
    sin = sin_ref[...]
    qn = qn_ref[...]
    kn = kn_ref[...]
    for h in range(N_HEADS):
        ph = _rope(_head_norm(p[:, h * HEAD_DIM:(h + 1) * HEAD_DIM], qn), cos, sin)
        for j in range(tok // T_Q):
            q_ref[0, h, j] = ph[j * T_Q:(j + 1) * T_Q].T.astype(BF16)
    for h in range(N_KV_HEADS):
        ph = p[:, qw + h * HEAD_DIM: qw + (h + 1) * HEAD_DIM]
        k_ref[0, h] = _rope(_head_norm(ph, kn), cos, sin).astype(BF16)
        pv = p[:, qw + kw + h * HEAD_DIM: qw + kw + (h + 1) * HEAD_DIM]
        for j in range(tok // KV_SUB):
            v_ref[0, h, j] = pv[j * KV_SUB:(j + 1) * KV_SUB].T.astype(BF16)

    ph = jnp.dot(hh, w_ref[:, hy0:], preferred_element_type=F32) + b_ref[:, hy0:]
    top = jnp.where(i == 0, 0.0, ph[:HALO_F32])
    bot = jnp.where(i == nt - 1, 0.0, ph[HALO_F32:])
    pbuf[0:HALO_F32, :] = top
    pbuf[HALO_F32:HALO_F32 + tok, :] = p[:, hy0:]
    pbuf[HALO_F32 + tok:, :] = bot
    u = (cw_ref[0:1, :] * pbuf[HALO_F32 - 1:HALO_F32 - 1 + tok, :]
         + cw_ref[1:2, :] * pbuf[HALO_F32:HALO_F32 + tok, :]
         + cw_ref[2:3, :] * pbuf[HALO_F32 + 1:HALO_F32 + 1 + tok, :]
         + cb_ref[...])
    pad = jnp.zeros((T_PITCH - SLAB, d_hy3), F32)
    for j in range(tok // SLAB):
        u_ref[0, j * T_PITCH:j * T_PITCH + SLAB, :] = u[j * SLAB:(j + 1) * SLAB]
        u_ref[0, j * T_PITCH + SLAB:(j + 1) * T_PITCH, :] = pad


def _inproj(x, mod, n1w, w, b, qn, kn, cos_t, sin_t, cw, cb, ctx_len):
    bsz, seq, d = x.shape
    tok = T_TOK
    nt = seq // tok
    width = w.shape[1]
    qw = N_HEADS * HEAD_DIM
    kw = N_KV_HEADS * HEAD_DIM
    d_hy3 = width - qw - 2 * kw
    hb = tok // HALO_F32
    n_hblk = seq // HALO_F32
    kernel = functools.partial(_inproj_kernel, tok=tok, d_hy3=d_hy3)
    rows_p = (seq // SLAB) * T_PITCH
    out_shape = (
        jax.ShapeDtypeStruct((bsz, N_HEADS, seq // T_Q, HEAD_DIM, T_Q), BF16),
        jax.ShapeDtypeStruct((bsz, N_KV_HEADS, seq + ctx_len, HEAD_DIM), BF16),
        jax.ShapeDtypeStruct((bsz, N_KV_HEADS, (seq + ctx_len) // KV_SUB, HEAD_DIM, KV_SUB), BF16),
        jax.ShapeDtypeStruct((bsz, rows_p, d_hy3), F32),
    )
    return pl.pallas_call(
        kernel,
        out_shape=out_shape,
        grid=(bsz, nt),
        in_specs=[
            pl.BlockSpec((1, HALO_F32, d), lambda b_, i: (b_, jnp.maximum(i * hb - 1, 0), 0)),
            pl.BlockSpec((1, tok, d), lambda b_, i: (b_, i, 0)),
            pl.BlockSpec((1, HALO_F32, d), lambda b_, i: (b_, jnp.minimum((i + 1) * hb, n_hblk - 1), 0)),
            pl.BlockSpec((1, N_MOD, d), lambda b_, i: (b_, 0, 0)),
            _const_spec((1, d)),
            _const_spec((d, width)),
            _const_spec((1, width)),
            _const_spec((1, HEAD_DIM)),
            _const_spec((1, HEAD_DIM)),
            pl.BlockSpec((tok, HEAD_DIM), lambda b_, i: (i, 0)),
            pl.BlockSpec((tok, HEAD_DIM), lambda b_, i: (i, 0)),
            _const_spec((3, d_hy3)),
            _const_spec((1, d_hy3)),
        ],
        out_specs=(
            pl.BlockSpec((1, N_HEADS, tok // T_Q, HEAD_DIM, T_Q), lambda b_, i: (b_, 0, i, 0, 0)),
            pl.BlockSpec((1, N_KV_HEADS, tok, HEAD_DIM), lambda b_, i: (b_, 0, i, 0)),
            pl.BlockSpec((1, N_KV_HEADS, tok // KV_SUB, HEAD_DIM, KV_SUB), lambda b_, i: (b_, 0, i, 0, 0)),
            pl.BlockSpec((1, (tok // SLAB) * T_PITCH, d_hy3), lambda b_, i: (b_, i, 0)),
        ),
        scratch_shapes=[pltpu.VMEM((tok + 2 * HALO_F32, d_hy3), F32)],
        compiler_params=pltpu.CompilerParams(
            dimension_semantics=("arbitrary", "arbitrary"), vmem_limit_bytes=VMEM_LIMIT),
        name="inproj",
    )(x, x, x, mod, n1w, w, b, qn, kn, cos_t, sin_t, cw, cb)


def _ctx_kv_kernel(c_ref, mod_ref, n1_ref, w_ref, b_ref, kn_ref, kin_ref, vin_ref, k_ref, v_ref):
    del kin_ref, vin_ref
    h = _mod_norm(c_ref[0], n1_ref[...], mod_ref[0, 0:1, :], mod_ref[0, 1:2, :]).astype(BF16)
    p = jnp.dot(h, w_ref[...], preferred_element_type=F32) + b_ref[...]
    kw = N_KV_HEADS * HEAD_DIM
    kn = kn_ref[...]
    for hd in range(N_KV_HEADS):
        k_ref[0, hd] = _head_norm(p[:, hd * HEAD_DIM:(hd + 1) * HEAD_DIM], kn).astype(BF16)
        pv = p[:, kw + hd * HEAD_DIM: kw + (hd + 1) * HEAD_DIM]
        for j in range(pv.shape[0] // KV_SUB):
            v_ref[0, hd, j] = pv[j * KV_SUB:(j + 1) * KV_SUB].T.astype(BF16)


def _ctx_kv(ctx, mod, n1w, w, b, kn, k_all, v_all, seq):
    bsz, ctx_len, d = ctx.shape
    ctx_row = bsz
    blk = seq // ctx_len
    any_spec = pl.BlockSpec(memory_space=pl.ANY)
    return pl.pallas_call(
        _ctx_kv_kernel,
        out_shape=(jax.ShapeDtypeStruct(k_all.shape, k_all.dtype),
                   jax.ShapeDtypeStruct(v_all.shape, v_all.dtype)),
        grid=(bsz,),
        in_specs=[
            pl.BlockSpec((1, ctx_len, d), lambda b_: (b_, 0, 0)),
            pl.BlockSpec((1, N_MOD, d), lambda b_: (ctx_row, 0, 0)),
            _const_spec((1, d)),
            _const_spec(w.shape),
            _const_spec(b.shape),
            _const_spec((1, HEAD_DIM)),
            any_spec, any_spec,
        ],
        out_specs=(
            pl.BlockSpec((1, N_KV_HEADS, ctx_len, HEAD_DIM), lambda b_: (b_, 0, blk, 0)),
            pl.BlockSpec((1, N_KV_HEADS, ctx_len // KV_SUB, HEAD_DIM, KV_SUB),
                         lambda b_: (b_, 0, seq // ctx_len, 0, 0)),
        ),
        input_output_aliases={6: 0, 7: 1},
        compiler_params=pltpu.CompilerParams(dimension_semantics=("arbitrary",)),
        name="ctx_kv",
    )(ctx, mod, n1w, w, b, kn, k_all, v_all)


def _attn_kernel(q_ref, k_ref, v_ref, o_ref, s_a, s_b, p_a, p_b, acc_ref, *, n_sub):
    tq = q_ref.shape[4]

    def scores(c, s_out):
        start = c * KV_SUB
        if not isinstance(c, int):
            start = pl.multiple_of(start, KV_SUB)
        kc = k_ref[0, 0, pl.ds(start, KV_SUB), :]
        for g in range(Q_GROUP):
            s_out[g] = jnp.dot(kc, q_ref[0, g, 0], preferred_element_type=F32)

    def softmax(s_in, p_out, stats):
        new_stats, alphas = [], []
        for g in range(Q_GROUP):
            m, l = stats[g]
            s = s_in[g]
            m_new = jnp.maximum(m, jnp.max(s, axis=0, keepdims=True))
            alpha = jnp.exp2(m - m_new)
            p = jnp.exp2(s - m_new)
            p_out[g] = p.astype(BF16)
            new_stats.append((m_new, alpha * l + jnp.sum(p, axis=0, keepdims=True)))
            alphas.append(alpha)
        return tuple(new_stats), tuple(alphas)

    def values(c, p_in, alphas):
        vt = v_ref[0, 0, c]
        for g in range(Q_GROUP):
            acc_ref[g] = alphas[g] * acc_ref[g] + jnp.dot(vt, p_in[g], preferred_element_type=F32)

    assert n_sub % 2 == 1 and n_sub >= 3
    acc_ref[...] = jnp.zeros(acc_ref.shape, F32)
    stats = ((jnp.full((1, tq), -jnp.inf, F32), jnp.zeros((1, tq), F32)),) * Q_GROUP
    scores(0, s_a)
    scores(1, s_b)
    stats, alphas = softmax(s_a, p_a, stats)

    def pair(t, carry):
        stats, alphas = carry
        c = 2 * t + 1
        scores(c + 1, s_a)
        stats, alphas_new = softmax(s_b, p_b, stats)
        values(c - 1, p_a, alphas)
        scores(jnp.minimum(c + 2, n_sub - 1), s_b)
        stats, alphas = softmax(s_a, p_a, stats)
        values(c, p_b, alphas_new)
        return stats, alphas

    stats, alphas = lax.fori_loop(0, (n_sub - 1) // 2, pair, (stats, alphas))
    values(n_sub - 1, p_a, alphas)
    for g in range(Q_GROUP):
        out = acc_ref[g] / stats[g][1]
        o_ref[0, :, g * HEAD_DIM:(g + 1) * HEAD_DIM] = out.T.astype(o_ref.dtype)


def _attention(q, k_all, v_all):
    bsz, _, n_qsub, _, tq = q.shape
    n_keys = k_all.shape[2]
    n_sub = v_all.shape[2]
    kernel = functools.partial(_attn_kernel, n_sub=n_sub)
    return pl.pallas_call(
        kernel,
        out_shape=jax.ShapeDtypeStruct((bsz, n_qsub * tq, N_HEADS * HEAD_DIM), BF16),
        grid=(bsz, N_KV_HEADS, n_qsub),
        in_specs=[
            pl.BlockSpec((1, Q_GROUP, 1, HEAD_DIM, tq), lambda b_, h, i: (b_, h, i, 0, 0)),
            pl.BlockSpec((1, 1, n_keys, HEAD_DIM), lambda b_, h, i: (b_, h, 0, 0)),
            pl.BlockSpec((1, 1, n_sub, HEAD_DIM, KV_SUB), lambda b_, h, i: (b_, h, 0, 0, 0)),
        ],
        out_specs=pl.BlockSpec((1, tq, Q_GROUP * HEAD_DIM), lambda b_, h, i: (b_, i, h)),
        scratch_shapes=[
            pltpu.VMEM((Q_GROUP, KV_SUB, tq), F32), pltpu.VMEM((Q_GROUP, KV_SUB, tq), F32),
            pltpu.VMEM((Q_GROUP, KV_SUB, tq), BF16), pltpu.VMEM((Q_GROUP, KV_SUB, tq), BF16),
            pltpu.VMEM((Q_GROUP, HEAD_DIM, tq), F32),
        ],
        compiler_params=pltpu.CompilerParams(
            dimension_semantics=("arbitrary", "arbitrary", "arbitrary"), vmem_limit_bytes=VMEM_LIMIT),
        name="attention",
    )(q, k_all, v_all)


def _filter_kernel(z_ref, tab_ref, dl_ref, w1_ref, b1_ref, wi_ref, bi_ref, fr_ref, wf_ref, wb_ref,
                   g_ref, *, rows, d_hy, n_inner):
    fr = fr_ref[...]
    h = jnp.sin(fr * (jnp.dot(z_ref[...], w1_ref[...], precision=HIGHEST,
                              preferred_element_type=F32) + b1_ref[...]))
    for j in range(n_inner):
        h = jnp.sin(fr * (jnp.dot(h, wi_ref[j], precision=HIGHEST,
                                  preferred_element_type=F32) + bi_ref[j]))
    pf = jnp.dot(h, wf_ref[...], precision=HIGHEST, preferred_element_type=F32)
    pb = jnp.dot(h, wb_ref[...], precision=HIGHEST, preferred_element_type=F32)
    t = tab_ref[:, 0:1]
    mf = tab_ref[:, 1:2]
    mb = tab_ref[:, 2:3]
    decay = jnp.exp(-t * dl_ref[...])
    pad = jnp.zeros((T_PITCH - SLAB, d_hy), F32)
    for o in range(HYENA_ORDER):
        g = (mf * pf[:, o * d_hy:(o + 1) * d_hy] + mb * pb[:, o * d_hy:(o + 1) * d_hy]) * decay
        for j in range(rows // SLAB):
            g_ref[o, j * T_PITCH:j * T_PITCH + SLAB, :] = g[j * SLAB:(j + 1) * SLAB]
            g_ref[o, j * T_PITCH + SLAB:(j + 1) * T_PITCH, :] = pad


def _filter_time(z2, tab, absdelta, w1p, b1, wi, bi, freq, wf, wb):
    n_rows = z2.shape[0]
    d_hy = absdelta.shape[1]
    rows = T_FILT
    n_inner = wi.shape[0]
    kernel = functools.partial(_filter_kernel, rows=rows, d_hy=d_hy, n_inner=n_inner)
    return pl.pallas_call(
        kernel,
        out_shape=jax.ShapeDtypeStruct((HYENA_ORDER, (n_rows // SLAB) * T_PITCH, d_hy), F32),
        grid=(n_rows // rows,),
        in_specs=[
            pl.BlockSpec((rows, z2.shape[1]), lambda i: (i, 0)),
            pl.BlockSpec((rows, tab.shape[1]), lambda i: (i, 0)),
            _const_spec(absdelta.shape), _const_spec(w1p.shape), _const_spec(b1.shape),
            _const_spec(wi.shape), _const_spec(bi.shape), _const_spec(freq.shape),
            _const_spec(wf.shape), _const_spec(wb.shape),
        ],
        out_specs=pl.BlockSpec((HYENA_ORDER, (rows // SLAB) * T_PITCH, d_hy), lambda i: (0, i, 0)),
        compiler_params=pltpu.CompilerParams(
            dimension_semantics=("arbitrary",), vmem_limit_bytes=VMEM_LIMIT),
        name="hyena_filter",
    )(z2, tab, absdelta, w1p, b1, wi, bi, freq, wf, wb)


def _loop(n, body, unroll):
    def group(t, carry):
        for j in range(unroll):
            body(t * unroll + j)
        return carry

    lax.fori_loop(0, n // unroll, group, 0)
    for i in range(n - n % unroll, n):
        body(i)


def _forward_cross_slab(src_ref, n_in, ff_ref, s_ref, half):
    def body(r):
        g = src_ref[pl.ds(r, n_in, stride=T_PITCH), :]
        a = jnp.dot(ff_ref[r], g.astype(BF16), preferred_element_type=F32)
        s_ref[pl.ds(r, half, stride=S_PITCH), :] = a[:half]
        s_ref[pl.ds(SLAB + r, half, stride=S_PITCH), :] = a[half:]

    _loop(SLAB, body, CROSS_UNROLL)


def _filter_spec_kernel(g_ref, ff_ref, fa_ref, k_ref, s_ref, *, n_in, n_p, half):
    _forward_cross_slab(g_ref.at[0], n_in, ff_ref, s_ref, half)
    fa = fa_ref[...]

    def slab(p):
        start = p * S_PITCH
        if not isinstance(p, int):
            start = pl.multiple_of(start, SUBLANES)
        a = s_ref[pl.ds(start, 2 * SLAB), :]
        k_ref[0, p] = jnp.dot(fa, a.astype(BF16), preferred_element_type=F32).astype(k_ref.dtype)

    _loop(n_p, slab, SLAB_UNROLL)


def _filter_spectrum(g, ff_full, fa, n_p):
    n_ord, rows_p, d_hy = g.shape
    n_in = rows_p // T_PITCH
    half = ff_full.shape[1] // 2
    kernel = functools.partial(_filter_spec_kernel, n_in=n_in, n_p=n_p, half=half)
    return pl.pallas_call(
        kernel,
        out_shape=jax.ShapeDtypeStruct((n_ord, n_p, 2 * SLAB, d_hy), BF16),
        grid=(n_ord, d_hy // LANES),
        in_specs=[
            pl.BlockSpec((1, rows_p, LANES), lambda o, c: (o, 0, c)),
            _const_spec(ff_full.shape),
            _const_spec(fa.shape),
        ],
        out_specs=pl.BlockSpec((1, n_p, 2 * SLAB, LANES), lambda o, c: (o, 0, 0, c)),
        scratch_shapes=[pltpu.VMEM((half * S_PITCH, LANES), F32)],
        compiler_params=pltpu.CompilerParams(
            dimension_semantics=("arbitrary", "arbitrary"), vmem_limit_bytes=VMEM_LIMIT),
        name="hyena_filter_spectrum",
    )(g, ff_full, fa)


def _conv_kernel(z_ref, x_ref, k_ref, sk_ref, ff_ref, fa_ref, fai_ref, gi_ref, o_ref, s_ref,
                 *, n_a, n_p, half):
    _forward_cross_slab(z_ref.at[0], n_a, ff_ref, s_ref, half)
    fa = fa_ref[...]
    fai = fai_ref[...]

    def slab(p):
        start = p * S_PITCH
        if not isinstance(p, int):
            start = pl.multiple_of(start, SUBLANES)
        a = s_ref[pl.ds(start, 2 * SLAB), :]
        x = jnp.dot(fa, a.astype(BF16), preferred_element_type=F32)
        kk = k_ref[0, p].astype(F32)
        xr, xi = x[:SLAB], x[SLAB:]
        kr, ki = kk[:SLAB], kk[SLAB:]
        y = jnp.concatenate([xr * kr - xi * ki, xr * ki + xi * kr], axis=0)
        s_ref[pl.ds(start, 2 * SLAB), :] = jnp.dot(fai, y.astype(BF16), preferred_element_type=F32)

    _loop(n_p, slab, SLAB_UNROLL)
    skip = sk_ref[0]

    def inverse(r):
        br = s_ref[pl.ds(r, half, stride=S_PITCH), :]
        bi = s_ref[pl.ds(SLAB + r, half, stride=S_PITCH), :]
        bb = jnp.concatenate([br, bi], axis=0).astype(BF16)
        y = jnp.dot(gi_ref[r], bb, preferred_element_type=F32)
        z = z_ref[0, pl.ds(r, n_a, stride=T_PITCH), :]
        gate = x_ref[0, pl.ds(r, n_a, stride=T_PITCH), :]
        o_ref[0, pl.ds(r, n_a, stride=T_PITCH), :] = gate * (y + skip * z)

    _loop(SLAB, inverse, CROSS_UNROLL)
    pad = jnp.zeros((T_PITCH - SLAB, LANES), F32)
    for a in range(n_a):
        o_ref[0, a * T_PITCH + SLAB:(a + 1) * T_PITCH, :] = pad


def _hyena_conv(zin, z_blk0, gate, g_blk0, kspec, order, skip, ff, fa, fai, gi, d_hy):
    bsz, rows_p, _ = zin.shape
    n_a = rows_p // T_PITCH
    n_p = kspec.shape[1]
    half = ff.shape[1] // 2
    kernel = functools.partial(_conv_kernel, n_a=n_a, n_p=n_p, half=half)
    return pl.pallas_call(
        kernel,
        out_shape=jax.ShapeDtypeStruct((bsz, rows_p, d_hy), F32),
        grid=(d_hy // LANES, bsz),
        in_specs=[
            pl.BlockSpec((1, rows_p, LANES), lambda c, b_: (b_, 0, z_blk0 + c)),
            pl.BlockSpec((1, rows_p, LANES), lambda c, b_: (b_, 0, g_blk0 + c),
                         pipeline_mode=pl.Buffered(1)),
            pl.BlockSpec((1, n_p, 2 * SLAB, LANES), lambda c, b_: (order, 0, 0, c),
                         pipeline_mode=pl.Buffered(1)),
            pl.BlockSpec((1, 1, LANES), lambda c, b_: (order, 0, c)),
            _const_spec(ff.shape), _const_spec(fa.shape), _const_spec(fai.shape),
            _const_spec(gi.shape),
        ],
        out_specs=pl.BlockSpec((1, rows_p, LANES), lambda c, b_: (b_, 0, c)),
        scratch_shapes=[pltpu.VMEM((half * S_PITCH, LANES), F32)],
        compiler_params=pltpu.CompilerParams(
            dimension_semantics=("arbitrary", "arbitrary"), vmem_limit_bytes=VMEM_LIMIT),
        name=f"hyena_conv{order}",
    )(zin, gate, kspec, skip, ff, fa, fai, gi)


def _merge_kernel(x_ref, mod_ref, n1_ref, n2_ref, wg_ref, bg_ref, a_ref, hy_ref, wa_ref, wh_ref,
                  wo_ref, bo_ref, xo_ref, h2_ref, *, tok, d):
    x = x_ref[0]
    sh1, sc1, g1 = mod_ref[0, 0:1, :], mod_ref[0, 1:2, :], mod_ref[0, 2:3, :]
    sh2, sc2 = mod_ref[0, 3:4, :], mod_ref[0, 4:5, :]
    h1 = _mod_norm(x, n1_ref[...], sh1, sc1).astype(BF16)
    gates = jnp.dot(h1, wg_ref[...], preferred_element_type=F32) + bg_ref[...]
    hy = jnp.concatenate([hy_ref[0, j * T_PITCH:j * T_PITCH + SLAB, :] for j in range(tok // SLAB)],
                         axis=0)
    pa = jnp.dot(a_ref[0], wa_ref[...], preferred_element_type=F32)
    ph = jnp.dot(hy.astype(BF16), wh_ref[...], preferred_element_type=F32)
    mixed = jax.nn.sigmoid(gates[:, :d]) * pa + jax.nn.sigmoid(gates[:, d:]) * ph
    y = jnp.dot(mixed.astype(BF16), wo_ref[...], preferred_element_type=F32) + bo_ref[...]
    xn = x + g1 * y
    xo_ref[0] = xn
    h2_ref[0] = _mod_norm(xn, n2_ref[...], sh2, sc2).astype(BF16)


def _merge(x, mod, n1w, n2w, wg, bg, attn_o, hy_o, wa, wh, wo, bo):
    bsz, seq, d = x.shape
    tok = T_TOK
    d_hy = hy_o.shape[2]
    kernel = functools.partial(_merge_kernel, tok=tok, d=d)
    return pl.pallas_call(
        kernel,
        out_shape=(jax.ShapeDtypeStruct((bsz, seq, d), F32), jax.ShapeDtypeStruct((bsz, seq, d), BF16)),
        grid=(bsz, seq // tok),
        in_specs=[
            pl.BlockSpec((1, tok, d), lambda b_, i: (b_, i, 0)),
            pl.BlockSpec((1, N_MOD, d), lambda b_, i: (b_, 0, 0)),
            _const_spec((1, d)), _const_spec((1, d)),
            _const_spec(wg.shape), _const_spec(bg.shape),
            pl.BlockSpec((1, tok, attn_o.shape[2]), lambda b_, i: (b_, i, 0)),
            pl.BlockSpec((1, (tok // SLAB) * T_PITCH, d_hy), lambda b_, i: (b_, i, 0)),
            _const_spec(wa.shape), _const_spec(wh.shape), _const_spec(wo.shape), _const_spec(bo.shape),
        ],
        out_specs=(pl.BlockSpec((1, tok, d), lambda b_, i: (b_, i, 0)),
                   pl.BlockSpec((1, tok, d), lambda b_, i: (b_, i, 0))),
        compiler_params=pltpu.CompilerParams(
            dimension_semantics=("arbitrary", "arbitrary"), vmem_limit_bytes=VMEM_LIMIT),
        name="merge",
    )(x, mod, n1w, n2w, wg, bg, attn_o, hy_o, wa, wh, wo, bo)


def _gelu_tanh(x):
    return 0.5 * x * (1.0 + jnp.tanh(math.sqrt(2.0 / math.pi) * (x + 0.044715 * (x * x * x))))


def _ffn_kernel(hp_ref, hm_ref, hn_ref, x_ref, mod_ref, wu_ref, bu_ref, cw_ref, cb_ref, wd_ref, bd_ref,
                fn_ref, o_ref, pa_buf, pg_buf, *, tok, d_ff):
    i = pl.program_id(1)
    nt = pl.num_programs(1)
    hext = jnp.concatenate([hp_ref[0], hm_ref[0], hn_ref[0]], axis=0)
    rows = tok + 2 * HALO_BF16
    row_id = lax.broadcasted_iota(jnp.int32, (rows, 1), 0)
    valid = jnp.logical_and(jnp.logical_or(i > 0, row_id >= HALO_BF16),
                            jnp.logical_or(i < nt - 1, row_id < HALO_BF16 + tok))
    acc = jnp.zeros((tok, o_ref.shape[2]), F32)
    lo = HALO_BF16

    def conv(buf, col0):
        return (cw_ref[0:1, col0:col0 + FF_CHUNK] * buf[lo - 1:lo - 1 + tok, :]
                + cw_ref[1:2, col0:col0 + FF_CHUNK] * buf[lo:lo + tok, :]
                + cw_ref[2:3, col0:col0 + FF_CHUNK] * buf[lo + 1:lo + 1 + tok, :]
                + cb_ref[:, col0:col0 + FF_CHUNK])

    for c in range(d_ff // FF_CHUNK):
        ca = c * FF_CHUNK
        cg = d_ff + c * FF_CHUNK
        pa = jnp.dot(hext, wu_ref[:, ca:ca + FF_CHUNK], preferred_element_type=F32) + bu_ref[:, ca:ca + FF_CHUNK]
        pg = jnp.dot(hext, wu_ref[:, cg:cg + FF_CHUNK], preferred_element_type=F32) + bu_ref[:, cg:cg + FF_CHUNK]
        pa_buf[...] = jnp.where(valid, pa, 0.0)
        pg_buf[...] = jnp.where(valid, pg, 0.0)
        act = _gelu_tanh(conv(pa_buf, ca)) * conv(pg_buf, cg)
        acc = acc + jnp.dot(act.astype(BF16), wd_ref[ca:ca + FF_CHUNK, :], preferred_element_type=F32)
    g2 = mod_ref[0, 5:6, :]
    xn = x_ref[0] + g2 * (acc + bd_ref[...])
    o_ref[0] = _rms(xn) * fn_ref[...]


def _ffn(h2, x_new, mod, wu, bu, cw, cb, wd, bd, fnw):
    bsz, seq, d = x_new.shape
    tok = T_TOK
    d_ff = wd.shape[0]
    hb = tok // HALO_BF16
    n_hblk = seq // HALO_BF16
    rows = tok + 2 * HALO_BF16
    kernel = functools.partial(_ffn_kernel, tok=tok, d_ff=d_ff)
    return pl.pallas_call(
        kernel,
        out_shape=jax.ShapeDtypeStruct((bsz, seq, d), F32),
        grid=(bsz, seq // tok),
        in_specs=[
            pl.BlockSpec((1, HALO_BF16, d), lambda b_, i: (b_, jnp.maximum(i * hb - 1, 0), 0)),
            pl.BlockSpec((1, tok, d), lambda b_, i: (b_, i, 0)),
            pl.BlockSpec((1, HALO_BF16, d), lambda b_, i: (b_, jnp.minimum((i + 1) * hb, n_hblk - 1), 0)),
            pl.BlockSpec((1, tok, d), lambda b_, i: (b_, i, 0)),
            pl.BlockSpec((1, N_MOD, d), lambda b_, i: (b_, 0, 0)),
            _const_spec(wu.shape), _const_spec(bu.shape), _const_spec(cw.shape), _const_spec(cb.shape),
            _const_spec(wd.shape), _const_spec(bd.shape), _const_spec(fnw.shape),
        ],
        out_specs=pl.BlockSpec((1, tok, d), lambda b_, i: (b_, i, 0)),
        scratch_shapes=[pltpu.VMEM((rows, FF_CHUNK), F32), pltpu.VMEM((rows, FF_CHUNK), F32)],
        compiler_params=pltpu.CompilerParams(
            dimension_semantics=("arbitrary", "arbitrary"), vmem_limit_bytes=VMEM_LIMIT),
        name="conv_ffn",
    )(h2, h2, h2, x_new, mod, wu, bu, cw, cb, wd, bd, fnw)


def _rope_tables(seq):
    half = HEAD_DIM // 2
    t = np.arange(seq)
    row = (t // GRID_W).astype(np.float64)
    col = (t % GRID_W).astype(np.float64)
    freqs = ROPE_THETA ** (-np.arange(0, half, 2, dtype=np.float64) / half)
    ang = np.concatenate([row[:, None] * freqs, col[:, None] * freqs], axis=-1)
    cos = np.cos(ang)
    sin = np.sin(ang)
    cos_t = np.concatenate([cos, cos], axis=-1)
    sin_t = np.concatenate([-sin, sin], axis=-1)
    return jnp.asarray(cos_t, F32), jnp.asarray(sin_t, F32)


def _filter_tables(seq, d_hy):
    n = 2 * seq
    j = np.arange(n)
    lag = np.where(j < seq, j, n - j)
    t = lag / max(seq - 1, 1)
    bands = (FILTER_EMB_DIM - 1) // 2
    f = np.linspace(1e-4, bands - 1, bands)
    wpos = 2.0 * math.pi * lag / seq
    z = np.concatenate([t[:, None], np.cos(wpos[:, None] * f), -np.sin(wpos[:, None] * f)], axis=-1)
    z2 = np.zeros((n, LANES), np.float64)
    z2[:, :FILTER_EMB_DIM] = z
    tab = np.zeros((n, SUBLANES), np.float64)
    tab[:, 0] = t
    tab[:, 1] = (j < seq)
    tab[:, 2] = np.logical_or(j == 0, j > seq)
    min_decay = math.log(DECAY_TARGET) / SLOW_DECAY_PCT
    max_decay = math.log(DECAY_TARGET) / FAST_DECAY_PCT
    absdelta = np.abs(np.linspace(min_decay, max_decay, d_hy))[None, :]
    return jnp.asarray(z2, F32), jnp.asarray(tab, F32), jnp.asarray(absdelta, F32)


def _dft_tables(seq):
    n_a = seq // SLAB
    n_s = 2 * n_a
    n = 2 * seq
    n_p = n_a + 1
    half = _round_up(n_p, SUBLANES)
    p = np.arange(half)
    live = (p < n_p).astype(np.float64)
    r = np.arange(SLAB)
    a = np.arange(n_s)
    theta = 2.0 * math.pi * p[None, :, None] * (SLAB * a[None, None, :] + r[:, None, None]) / n
    ff = np.concatenate([np.cos(theta) * live[None, :, None], -np.sin(theta) * live[None, :, None]], axis=1)
    phi = 2.0 * math.pi * np.outer(r, r) / SLAB
    c, s = np.cos(phi), np.sin(phi)
    fa = np.block([[c, s], [-s, c]])
    fai = np.block([[c, -s], [s, c]])
    wgt = np.where((p == 0) | (p == n_a), 1.0, 2.0) * live / n
    th_i = 2.0 * math.pi * (SLAB * a[None, :n_a, None] + r[:, None, None]) * p[None, None, :] / n
    gi = np.concatenate([np.cos(th_i) * wgt[None, None, :], -np.sin(th_i) * wgt[None, None, :]], axis=2)
    to_bf = lambda m: jnp.asarray(m, F32).astype(BF16)
    return to_bf(ff), to_bf(ff[:, :, :n_a]), to_bf(fa), to_bf(fai), to_bf(gi), n_p


def kernel(x, c, ctx, c_ctx, w_mod, b_mod, norm1_w, norm2_w, w_in, b_in, q_norm_w, k_norm_w, hy_conv_w, hy_conv_b, filt_w1, filt_b1, filt_w_inner, filt_b_inner, filt_freq, filt_w_out, hy_skip, w_attn_out, w_hy_out, w_o, b_o, w_up, b_up, ffn_conv_w, ffn_conv_b, w_down, b_down, final_norm_w):
    bsz, seq, d = x.shape
    ctx_len = ctx.shape[1]
    d_hy = hy_skip.shape[2]
    assert w_mod.shape[0] == 1, "single-layer kernel"
    assert seq % T_TOK == 0 and seq % T_Q == 0 and seq % ctx_len == 0 and (2 * seq) % T_FILT == 0
    assert T_TOK % KV_SUB == 0 and ctx_len % KV_SUB == 0
    assert d_hy % LANES == 0 and w_down.shape[1] % FF_CHUNK == 0 and bsz + 1 <= SUBLANES

    cvec = jnp.zeros((SUBLANES, d), F32).at[:bsz].set(c).at[bsz].set(c_ctx)
    mod = _modulation(cvec, w_mod[0], b_mod[0]).reshape(SUBLANES, N_MOD, d)

    qw = N_HEADS * HEAD_DIM
    kw = N_KV_HEADS * HEAD_DIM
    hy0 = qw + 2 * kw
    g0 = hy0 + (HYENA_ORDER + 1) * d_hy
    perm = np.concatenate([np.arange(0, HEAD_DIM, 2), np.arange(1, HEAD_DIM, 2)])
    perm_q = (np.arange(N_HEADS)[:, None] * HEAD_DIM + perm[None, :]).reshape(-1)
    perm_k = qw + (np.arange(N_KV_HEADS)[:, None] * HEAD_DIM + perm[None, :]).reshape(-1)
    cols = np.concatenate([perm_q, perm_k, np.arange(qw + kw, g0)])
    w_main = w_in[0][:, cols].astype(BF16)
    b_main = b_in[0][cols].reshape(1, -1)
    w_kv = w_main[:, qw:hy0]
    b_kv = b_main[:, qw:hy0]
    w_gates = w_in[0][:, g0:].astype(BF16)
    b_gates = b_in[0][g0:].reshape(1, -1)
    qn = (q_norm_w[0][perm] * (math.log2(math.e) * HEAD_DIM ** -0.5)).reshape(1, HEAD_DIM)
    kn = k_norm_w[0][perm].reshape(1, HEAD_DIM)
    cos_t, sin_t = _rope_tables(seq)
    n1w = norm1_w[0].reshape(1, d)
    n2w = norm2_w[0].reshape(1, d)

    q, k_all, v_all, u = _inproj(x, mod, n1w, w_main, b_main, qn, kn, cos_t, sin_t,
                                 hy_conv_w[0], hy_conv_b[0].reshape(1, -1), ctx_len)
    k_all, v_all = _ctx_kv(ctx, mod, n1w, w_kv, b_kv, kn, k_all, v_all, seq)
    attn_o = _attention(q, k_all, v_all)

    z2, tab, absdelta = _filter_tables(seq, d_hy)
    w1p = jnp.zeros((LANES, filt_w1.shape[2]), F32).at[:FILTER_EMB_DIM].set(filt_w1[0])
    w_out = filt_w_out[0].reshape(-1, 2, HYENA_ORDER * d_hy)
    g_time = _filter_time(z2, tab, absdelta, w1p, filt_b1[0].reshape(1, -1), filt_w_inner[0],
                          filt_b_inner[0][:, None, :], filt_freq[0].reshape(1, -1),
                          w_out[:, 0], w_out[:, 1])
    ff_full, ff_half, fa, fai, gi, n_p = _dft_tables(seq)
    kspec = _filter_spectrum(g_time, ff_full, fa, n_p)
    skip = hy_skip[0].reshape(HYENA_ORDER, 1, d_hy)
    lanes_per_stream = d_hy // LANES
    z1 = _hyena_conv(u, 2 * lanes_per_stream, u, 0, kspec, 0, skip, ff_half, fa, fai, gi, d_hy)
    hy_o = _hyena_conv(z1, 0, u, lanes_per_stream, kspec, 1, skip, ff_half, fa, fai, gi, d_hy)

    x_new, h2 = _merge(x, mod, n1w, n2w, w_gates, b_gates, attn_o, hy_o,
                       w_attn_out[0].astype(BF16), w_hy_out[0].astype(BF16), w_o[0].astype(BF16),
                       b_o[0].reshape(1, d))
    return _ffn(h2, x_new, mod, w_up[0].astype(BF16), b_up[0].reshape(1, -1), ffn_conv_w[0],
                ffn_conv_b[0].reshape(1, -1), w_down[0].astype(BF16), b_down[0].reshape(1, d),
                final_norm_w.reshape(1, d))
```

```python
import functools
import math

import numpy as np
import jax
import jax.numpy as jnp
from jax import lax
from jax.experimental import pallas as pl
from jax.experimental.pallas import tpu as pltpu

F32 = jnp.float32
BF16 = jnp.bfloat16
HIGHEST = lax.Precision.HIGHEST

GRID_W = 64
N_HEADS = 8
N_KV_HEADS = 2
Q_GROUP = N_HEADS // N_KV_HEADS
HEAD_DIM = 128
ROPE_THETA = 10000.0
RMS_EPS = 1e-6
HYENA_ORDER = 2
FILTER_EMB_DIM = 33
DECAY_TARGET = 1e-2
FAST_DECAY_PCT = 0.3
SLOW_DECAY_PCT = 1.5
N_MOD = 6

LANES = 128
SUBLANES = 8
BF16_ROWS = 16
VMEM_LIMIT = 56 * 1024 * 1024

SLAB = 128
T_PITCH = SLAB + SUBLANES
S_PITCH = 2 * SLAB + SUBLANES

T_TOK = 512
T_Q = 256
KV_SUB = 256
V_ROWS = HEAD_DIM + BF16_ROWS
ATTN_STEPS = 8
CROSS_UNROLL = 8
SLAB_UNROLL = 4
T_FILT = 1024
FF_CHUNK = 256
HALO_F32 = SUBLANES
HALO_BF16 = BF16_ROWS


def _round_up(n, m):
    return (n + m - 1) // m * m


def _rms(x):
    return x * lax.rsqrt(jnp.mean(x * x, axis=-1, keepdims=True) + RMS_EPS)


def _mod_norm(x, w, shift, scale):
    return (_rms(x) * w) * (1.0 + scale) + shift


def _const_spec(shape):
    n = len(shape)
    return pl.BlockSpec(shape, lambda *_: (0,) * n, pipeline_mode=pl.Buffered(1))


def _modulation_kernel(c_ref, w_ref, b_ref, o_ref):
    c = c_ref[...]
    s = c * jax.nn.sigmoid(c)
    o_ref[...] = jnp.dot(s, w_ref[...], precision=HIGHEST, preferred_element_type=F32) + b_ref[...]


def _modulation(cvec, w_mod, b_mod):
    rows, d = cvec.shape
    width = w_mod.shape[1]
    blk = d
    return pl.pallas_call(
        _modulation_kernel,
        out_shape=jax.ShapeDtypeStruct((rows, width), F32),
        grid=(width // blk,),
        in_specs=[pl.BlockSpec((rows, d), lambda j: (0, 0)),
                  pl.BlockSpec((d, blk), lambda j: (0, j)),
                  pl.BlockSpec((1, blk), lambda j: (0, j))],
        out_specs=pl.BlockSpec((rows, blk), lambda j: (0, j)),
        compiler_params=pltpu.CompilerParams(dimension_semantics=("arbitrary",)),
        name="modulation",
    )(cvec, w_mod, b_mod.reshape(1, width))


def _head_norm(p, w):
    return _rms(p) * w


def _values_block(v):
    ones = jnp.ones((V_ROWS - HEAD_DIM, v.shape[0]), F32)
    return jnp.concatenate([v.T, ones], axis=0).astype(BF16)


def _rope(p, cos, sin):
    return p * cos + pltpu.roll(p, HEAD_DIM // 2, axis=1) * sin


def _inproj_kernel(xp_ref, xm_ref, xn_ref, mod_ref, n1_ref, w_ref, b_ref, qn_ref, kn_ref,
                   cos_ref, sin_ref, cw_ref, cb_ref,
                   q_ref, k_ref, v_ref, u_ref, pbuf, *, tok, d_hy3):
    i = pl.program_id(1)
    nt = pl.num_programs(1)
    shift = mod_ref[0, 0:1, :]
    scale = mod_ref[0, 1:2, :]
    n1 = n1_ref[...]
    hm = _mod_norm(xm_ref[0], n1, shift, scale).astype(BF16)
    halo = jnp.concatenate([xp_ref[0], xn_ref[0]], axis=0)
    hh = _mod_norm(halo, n1, shift, scale).astype(BF16)

    qw = N_HEADS * HEAD_DIM
    kw = N_KV_HEADS * HEAD_DIM
    hy0 = qw + 2 * kw
    cos = cos_ref[...]
    sin = sin_ref[...]
    qn = qn_ref[...]
    kn = kn_ref[...]

    p = jnp.dot(hm, w_ref[...], preferred_element_type=F32) + b_ref[...]
    for h in range(N_HEADS):
        ph = _rope(_head_norm(p[:, h * HEAD_DIM:(h + 1) * HEAD_DIM], qn), cos, sin)
        for j in range(tok // T_Q):
            q_ref[0, h, j] = ph[j * T_Q:(j + 1) * T_Q].T.astype(BF16)
    for h in range(N_KV_HEADS):
        ph = p[:, qw + h * HEAD_DIM: qw + (h + 1) * HEAD_DIM]
        k_ref[0, h] = _rope(_head_norm(ph, kn), cos, sin).astype(BF16)
        pv = p[:, qw + kw + h * HEAD_DIM: qw + kw + (h + 1) * HEAD_DIM]
        for j in range(tok // KV_SUB):
            v_ref[0, h, j] = _values_block(pv[j * KV_SUB:(j + 1) * KV_SUB])

    ph = jnp.dot(hh, w_ref[:, hy0:], preferred_element_type=F32) + b_ref[:, hy0:]
    top = jnp.where(i == 0, 0.0, ph[:HALO_F32])
    bot = jnp.where(i == nt - 1, 0.0, ph[HALO_F32:])
    pbuf[0:HALO_F32, :] = top
    pbuf[HALO_F32:HALO_F32 + tok, :] = p[:, hy0:]
    pbuf[HALO_F32 + tok:, :] = bot
    u = (cw_ref[0:1, :] * pbuf[HALO_F32 - 1:HALO_F32 - 1 + tok, :]
         + cw_ref[1:2, :] * pbuf[HALO_F32:HALO_F32 + tok, :]
         + cw_ref[2:3, :] * pbuf[HALO_F32 + 1:HALO_F32 + 1 + tok, :]
         + cb_ref[...])
    pad = jnp.zeros((T_PITCH - SLAB, d_hy3), F32)
    for j in range(tok // SLAB):
        u_ref[0, j * T_PITCH:j * T_PITCH + SLAB, :] = u[j * SLAB:(j + 1) * SLAB]
        u_ref[0, j * T_PITCH + SLAB:(j + 1) * T_PITCH, :] = pad


def _inproj(x, mod, n1w, w, b, qn, kn, cos_t, sin_t, cw, cb, ctx_len):
    bsz, seq, d = x.shape
    tok = T_TOK
    nt = seq // tok
    width = w.shape[1]
    qw = N_HEADS * HEAD_DIM
    kw = N_KV_HEADS * HEAD_DIM
    d_hy3 = width - qw - 2 * kw
    hb = tok // HALO_F32
    n_hblk = seq // HALO_F32
    kernel = functools.partial(_inproj_kernel, tok=tok, d_hy3=d_hy3)
    rows_p = (seq // SLAB) * T_PITCH
    out_shape = (
        jax.ShapeDtypeStruct((bsz, N_HEADS, seq // T_Q, HEAD_DIM, T_Q), BF16),
        jax.ShapeDtypeStruct((bsz, N_KV_HEADS, seq + ctx_len, HEAD_DIM), BF16),
        jax.ShapeDtypeStruct((bsz, N_KV_HEADS, (seq + ctx_len) // KV_SUB, V_ROWS, KV_SUB), BF16),
        jax.ShapeDtypeStruct((bsz, rows_p, d_hy3), F32),
    )
    return pl.pallas_call(
        kernel,
        out_shape=out_shape,
        grid=(bsz, nt),
        in_specs=[
            pl.BlockSpec((1, HALO_F32, d), lambda b_, i: (b_, jnp.maximum(i * hb - 1, 0), 0)),
            pl.BlockSpec((1, tok, d), lambda b_, i: (b_, i, 0)),
            pl.BlockSpec((1, HALO_F32, d), lambda b_, i: (b_, jnp.minimum((i + 1) * hb, n_hblk - 1), 0)),
            pl.BlockSpec((1, N_MOD, d), lambda b_, i: (b_, 0, 0)),
            _const_spec((1, d)),
            _const_spec((d, width)),
            _const_spec((1, width)),
            _const_spec((1, HEAD_DIM)),
            _const_spec((1, HEAD_DIM)),
            pl.BlockSpec((tok, HEAD_DIM), lambda b_, i: (i, 0)),
            pl.BlockSpec((tok, HEAD_DIM), lambda b_, i: (i, 0)),
            _const_spec((3, d_hy3)),
            _const_spec((1, d_hy3)),
        ],
        out_specs=(
            pl.BlockSpec((1, N_HEADS, tok // T_Q, HEAD_DIM, T_Q), lambda b_, i: (b_, 0, i, 0, 0)),
            pl.BlockSpec((1, N_KV_HEADS, tok, HEAD_DIM), lambda b_, i: (b_, 0, i, 0)),
            pl.BlockSpec((1, N_KV_HEADS, tok // KV_SUB, V_ROWS, KV_SUB), lambda b_, i: (b_, 0, i, 0, 0)),
            pl.BlockSpec((1, (tok // SLAB) * T_PITCH, d_hy3), lambda b_, i: (b_, i, 0)),
        ),
        scratch_shapes=[pltpu.VMEM((tok + 2 * HALO_F32, d_hy3), F32)],
        compiler_params=pltpu.CompilerParams(
            dimension_semantics=("arbitrary", "arbitrary"), vmem_limit_bytes=VMEM_LIMIT),
        name="inproj",
    )(x, x, x, mod, n1w, w, b, qn, kn, cos_t, sin_t, cw, cb)


def _ctx_kv_kernel(c_ref, mod_ref, n1_ref, w_ref, b_ref, kn_ref, kin_ref, vin_ref, k_ref, v_ref):
    del kin_ref, vin_ref
    h = _mod_norm(c_ref[0], n1_ref[...], mod_ref[0, 0:1, :], mod_ref[0, 1:2, :]).astype(BF16)
    p = jnp.dot(h, w_ref[...], preferred_element_type=F32) + b_ref[...]
    kw = N_KV_HEADS * HEAD_DIM
    kn = kn_ref[...]
    for hd in range(N_KV_HEADS):
        k_ref[0, hd] = _head_norm(p[:, hd * HEAD_DIM:(hd + 1) * HEAD_DIM], kn).astype(BF16)
        pv = p[:, kw + hd * HEAD_DIM: kw + (hd + 1) * HEAD_DIM]
        for j in range(pv.shape[0] // KV_SUB):
            v_ref[0, hd, j] = _values_block(pv[j * KV_SUB:(j + 1) * KV_SUB])


def _ctx_kv(ctx, mod, n1w, w, b, kn, k_all, v_all, seq):
    bsz, ctx_len, d = ctx.shape
    ctx_row = bsz
    blk = seq // ctx_len
    any_spec = pl.BlockSpec(memory_space=pl.ANY)
    return pl.pallas_call(
        _ctx_kv_kernel,
        out_shape=(jax.ShapeDtypeStruct(k_all.shape, k_all.dtype),
                   jax.ShapeDtypeStruct(v_all.shape, v_all.dtype)),
        grid=(bsz,),
        in_specs=[
            pl.BlockSpec((1, ctx_len, d), lambda b_: (b_, 0, 0)),
            pl.BlockSpec((1, N_MOD, d), lambda b_: (ctx_row, 0, 0)),
            _const_spec((1, d)),
            _const_spec(w.shape),
            _const_spec(b.shape),
            _const_spec((1, HEAD_DIM)),
            any_spec, any_spec,
        ],
        out_specs=(
            pl.BlockSpec((1, N_KV_HEADS, ctx_len, HEAD_DIM), lambda b_: (b_, 0, blk, 0)),
            pl.BlockSpec((1, N_KV_HEADS, ctx_len // KV_SUB, V_ROWS, KV_SUB),
                         lambda b_: (b_, 0, seq // ctx_len, 0, 0)),
        ),
        input_output_aliases={6: 0, 7: 1},
        compiler_params=pltpu.CompilerParams(dimension_semantics=("arbitrary",)),
        name="ctx_kv",
    )(ctx, mod, n1w, w, b, kn, k_all, v_all)


def _attn_kernel(q_ref, k_ref, v_ref, o_ref, s_a, s_b, p_a, p_b, acc_ref, *, n_sub):
    tq = q_ref.shape[4]

    def scores(c, s_out):
        start = c * KV_SUB
        if not isinstance(c, int):
            start = pl.multiple_of(start, KV_SUB)
        kc = k_ref[0, 0, pl.ds(start, KV_SUB), :]
        for g in range(Q_GROUP):
            s_out[g] = jnp.dot(kc, q_ref[0, g, 0], preferred_element_type=F32)

    def softmax(s_in, p_out, stats):
        new_stats, alphas = [], []
        for g in range(Q_GROUP):
            s = s_in[g]
            m_new = jnp.maximum(stats[g], jnp.max(s, axis=0, keepdims=True))
            alphas.append(jnp.exp2(stats[g] - m_new))
            p_out[g] = jnp.exp2(s - m_new).astype(BF16)
            new_stats.append(m_new)
        return tuple(new_stats), tuple(alphas)

    def values(c, p_in, alphas):
        vt = v_ref[0, 0, c]
        for g in range(Q_GROUP):
            acc_ref[g] = alphas[g] * acc_ref[g] + jnp.dot(vt, p_in[g], preferred_element_type=F32)

    assert n_sub % 2 == 1 and n_sub >= 3
    acc_ref[...] = jnp.zeros(acc_ref.shape, F32)
    stats = (jnp.full((1, tq), -jnp.inf, F32),) * Q_GROUP
    scores(0, s_a)
    scores(1, s_b)
    stats, alphas = softmax(s_a, p_a, stats)

    def group(t, carry):
        stats, alphas = carry
        for u in range(ATTN_STEPS):
            c = ATTN_STEPS * t + 1 + u
            s_next, s_cur, p_cur, p_prev = (s_a, s_b, p_b, p_a) if u % 2 == 0 else (s_b, s_a, p_a, p_b)
            scores(jnp.minimum(c + 1, n_sub - 1), s_next)
            stats, alphas_new = softmax(s_cur, p_cur, stats)
            values(c - 1, p_prev, alphas)
            alphas = alphas_new
        return stats, alphas

    assert (n_sub - 1) % ATTN_STEPS == 0 and ATTN_STEPS % 2 == 0
    stats, alphas = lax.fori_loop(0, (n_sub - 1) // ATTN_STEPS, group, (stats, alphas))
    values(n_sub - 1, p_a, alphas)
    for g in range(Q_GROUP):
        out = acc_ref[g, :HEAD_DIM, :] / acc_ref[g, HEAD_DIM:HEAD_DIM + 1, :]
        o_ref[0, :, g * HEAD_DIM:(g + 1) * HEAD_DIM] = out.T.astype(o_ref.dtype)


def _attention(q, k_all, v_all):
    bsz, _, n_qsub, _, tq = q.shape
    n_keys = k_all.shape[2]
    n_sub = v_all.shape[2]
    kernel = functools.partial(_attn_kernel, n_sub=n_sub)
    return pl.pallas_call(
        kernel,
        out_shape=jax.ShapeDtypeStruct((bsz, n_qsub * tq, N_HEADS * HEAD_DIM), BF16),
        grid=(bsz, N_KV_HEADS, n_qsub),
        in_specs=[
            pl.BlockSpec((1, Q_GROUP, 1, HEAD_DIM, tq), lambda b_, h, i: (b_, h, i, 0, 0)),
            pl.BlockSpec((1, 1, n_keys, HEAD_DIM), lambda b_, h, i: (b_, h, 0, 0)),
            pl.BlockSpec((1, 1, n_sub, V_ROWS, KV_SUB), lambda b_, h, i: (b_, h, 0, 0, 0)),
        ],
        out_specs=pl.BlockSpec((1, tq, Q_GROUP * HEAD_DIM), lambda b_, h, i: (b_, i, h)),
        scratch_shapes=[
            pltpu.VMEM((Q_GROUP, KV_SUB, tq), F32), pltpu.VMEM((Q_GROUP, KV_SUB, tq), F32),
            pltpu.VMEM((Q_GROUP, KV_SUB, tq), BF16), pltpu.VMEM((Q_GROUP, KV_SUB, tq), BF16),
            pltpu.VMEM((Q_GROUP, V_ROWS, tq), F32),
        ],
        compiler_params=pltpu.CompilerParams(
            dimension_semantics=("arbitrary", "arbitrary", "arbitrary"), vmem_limit_bytes=VMEM_LIMIT),
        name="attention",
    )(q, k_all, v_all)


def _filter_kernel(z_ref, tab_ref, dl_ref, w1_ref, b1_ref, wi_ref, bi_ref, fr_ref, wf_ref, wb_ref,
                   g_ref, *, rows, d_hy, n_inner):
    fr = fr_ref[...]
    h = jnp.sin(fr * (jnp.dot(z_ref[...], w1_ref[...], precision=HIGHEST,
                              preferred_element_type=F32) + b1_ref[...]))
    for j in range(n_inner):
        h = jnp.sin(fr * (jnp.dot(h, wi_ref[j], precision=HIGHEST,
                                  preferred_element_type=F32) + bi_ref[j]))
    hb = h.astype(BF16)
    pf = jnp.dot(hb, wf_ref[...], preferred_element_type=F32)
    pb = jnp.dot(hb, wb_ref[...], preferred_element_type=F32)
    t = tab_ref[:, 0:1]
    mf = tab_ref[:, 1:2]
    mb = tab_ref[:, 2:3]
    decay = jnp.exp(-t * dl_ref[...])
    pad = jnp.zeros((T_PITCH - SLAB, d_hy), F32)
    for o in range(HYENA_ORDER):
        g = (mf * pf[:, o * d_hy:(o + 1) * d_hy] + mb * pb[:, o * d_hy:(o + 1) * d_hy]) * decay
        for j in range(rows // SLAB):
            g_ref[o, j * T_PITCH:j * T_PITCH + SLAB, :] = g[j * SLAB:(j + 1) * SLAB]
            g_ref[o, j * T_PITCH + SLAB:(j + 1) * T_PITCH, :] = pad


def _filter_time(z2, tab, absdelta, w1p, b1, wi, bi, freq, wf, wb):
    n_rows = z2.shape[0]
    d_hy = absdelta.shape[1]
    rows = T_FILT
    n_inner = wi.shape[0]
    kernel = functools.partial(_filter_kernel, rows=rows, d_hy=d_hy, n_inner=n_inner)
    return pl.pallas_call(
        kernel,
        out_shape=jax.ShapeDtypeStruct((HYENA_ORDER, (n_rows // SLAB) * T_PITCH, d_hy), F32),
        grid=(n_rows // rows,),
        in_specs=[
            pl.BlockSpec((rows, z2.shape[1]), lambda i: (i, 0)),
            pl.BlockSpec((rows, tab.shape[1]), lambda i: (i, 0)),
            _const_spec(absdelta.shape), _const_spec(w1p.shape), _const_spec(b1.shape),
            _const_spec(wi.shape), _const_spec(bi.shape), _const_spec(freq.shape),
            _const_spec(wf.shape), _const_spec(wb.shape),
        ],
        out_specs=pl.BlockSpec((HYENA_ORDER, (rows // SLAB) * T_PITCH, d_hy), lambda i: (0, i, 0)),
        compiler_params=pltpu.CompilerParams(
            dimension_semantics=("arbitrary",), vmem_limit_bytes=VMEM_LIMIT),
        name="hyena_filter",
    )(z2, tab, absdelta, w1p, b1, wi, bi, freq, wf, wb)


def _loop(n, body, unroll):
    def group(t, carry):
        for j in range(unroll):
            body(t * unroll + j)
        return carry

    lax.fori_loop(0, n // unroll, group, 0)
    for i in range(n - n % unroll, n):
        body(i)


def _forward_cross_slab(src_ref, n_in, ff_ref, s_ref, half):
    def body(r):
        g = src_ref[pl.ds(r, n_in, stride=T_PITCH), :]
        a = jnp.dot(ff_ref[r], g.astype(BF16), preferred_element_type=F32)
        s_ref[pl.ds(r, half, stride=S_PITCH), :] = a[:half]
        s_ref[pl.ds(SLAB + r, half, stride=S_PITCH), :] = a[half:]

    _loop(SLAB, body, CROSS_UNROLL)


def _filter_spec_kernel(g_ref, ff_ref, fa_ref, k_ref, s_ref, *, n_in, n_p, half):
    _forward_cross_slab(g_ref.at[0], n_in, ff_ref, s_ref, half)
    fa = fa_ref[...]

    def slab(p):
        start = p * S_PITCH
        if not isinstance(p, int):
            start = pl.multiple_of(start, SUBLANES)
        a = s_ref[pl.ds(start, 2 * SLAB), :]
        k_ref[0, p] = jnp.dot(fa, a.astype(BF16), preferred_element_type=F32).astype(k_ref.dtype)

    _loop(n_p, slab, SLAB_UNROLL)


def _filter_spectrum(g, ff_full, fa, n_p):
    n_ord, rows_p, d_hy = g.shape
    n_in = rows_p // T_PITCH
    half = ff_full.shape[1] // 2
    kernel = functools.partial(_filter_spec_kernel, n_in=n_in, n_p=n_p, half=half)
    return pl.pallas_call(
        kernel,
        out_shape=jax.ShapeDtypeStruct((n_ord, n_p, 2 * SLAB, d_hy), BF16),
        grid=(n_ord, d_hy // LANES),
        in_specs=[
            pl.BlockSpec((1, rows_p, LANES), lambda o, c: (o, 0, c)),
            _const_spec(ff_full.shape),
            _const_spec(fa.shape),
        ],
        out_specs=pl.BlockSpec((1, n_p, 2 * SLAB, LANES), lambda o, c: (o, 0, 0, c)),
        scratch_shapes=[pltpu.VMEM((half * S_PITCH, LANES), F32)],
        compiler_params=pltpu.CompilerParams(
            dimension_semantics=("arbitrary", "arbitrary"), vmem_limit_bytes=VMEM_LIMIT),
        name="hyena_filter_spectrum",
    )(g, ff_full, fa)


def _conv_kernel(z_ref, x_ref, k_ref, sk_ref, ff_ref, fa_ref, fai_ref, gi_ref, o_ref, s_ref,
                 *, n_a, n_p, half):
    _forward_cross_slab(z_ref.at[0], n_a, ff_ref, s_ref, half)
    fa = fa_ref[...]
    fai = fai_ref[...]

    def slab(p):
        start = p * S_PITCH
        if not isinstance(p, int):
            start = pl.multiple_of(start, SUBLANES)
        a = s_ref[pl.ds(start, 2 * SLAB), :]
        x = jnp.dot(fa, a.astype(BF16), preferred_element_type=F32)
        kk = k_ref[0, p].astype(F32)
        xr, xi = x[:SLAB], x[SLAB:]
        kr, ki = kk[:SLAB], kk[SLAB:]
        y = jnp.concatenate([xr * kr - xi * ki, xr * ki + xi * kr], axis=0)
        s_ref[pl.ds(start, 2 * SLAB), :] = jnp.dot(fai, y.astype(BF16), preferred_element_type=F32)

    _loop(n_p, slab, SLAB_UNROLL)
    skip = sk_ref[0]

    def inverse(r):
        br = s_ref[pl.ds(r, half, stride=S_PITCH), :]
        bi = s_ref[pl.ds(SLAB + r, half, stride=S_PITCH), :]
        bb = jnp.concatenate([br, bi], axis=0).astype(BF16)
        y = jnp.dot(gi_ref[r], bb, preferred_element_type=F32)
        z = z_ref[0, pl.ds(r, n_a, stride=T_PITCH), :]
        gate = x_ref[0, pl.ds(r, n_a, stride=T_PITCH), :]
        o_ref[0, pl.ds(r, n_a, stride=T_PITCH), :] = gate * (y + skip * z)

    _loop(SLAB, inverse, CROSS_UNROLL)
    pad = jnp.zeros((T_PITCH - SLAB, LANES), F32)
    for a in range(n_a):
        o_ref[0, a * T_PITCH + SLAB:(a + 1) * T_PITCH, :] = pad


def _hyena_conv(zin, z_blk0, gate, g_blk0, kspec, order, skip, ff, fa, fai, gi, d_hy):
    bsz, rows_p, _ = zin.shape
    n_a = rows_p // T_PITCH
    n_p = kspec.shape[1]
    half = ff.shape[1] // 2
    kernel = functools.partial(_conv_kernel, n_a=n_a, n_p=n_p, half=half)
    return pl.pallas_call(
        kernel,
        out_shape=jax.ShapeDtypeStruct((bsz, rows_p, d_hy), F32),
        grid=(d_hy // LANES, bsz),
        in_specs=[
            pl.BlockSpec((1, rows_p, LANES), lambda c, b_: (b_, 0, z_blk0 + c)),
            pl.BlockSpec((1, rows_p, LANES), lambda c, b_: (b_, 0, g_blk0 + c),
                         pipeline_mode=pl.Buffered(1)),
            pl.BlockSpec((1, n_p, 2 * SLAB, LANES), lambda c, b_: (order, 0, 0, c),
                         pipeline_mode=pl.Buffered(1)),
            pl.BlockSpec((1, 1, LANES), lambda c, b_: (order, 0, c)),
            _const_spec(ff.shape), _const_spec(fa.shape), _const_spec(fai.shape),
            _const_spec(gi.shape),
        ],
        out_specs=pl.BlockSpec((1, rows_p, LANES), lambda c, b_: (b_, 0, c)),
        scratch_shapes=[pltpu.VMEM((half * S_PITCH, LANES), F32)],
        compiler_params=pltpu.CompilerParams(
            dimension_semantics=("arbitrary", "arbitrary"), vmem_limit_bytes=VMEM_LIMIT),
        name=f"hyena_conv{order}",
    )(zin, gate, kspec, skip, ff, fa, fai, gi)


def _merge_kernel(x_ref, mod_ref, n1_ref, n2_ref, wg_ref, bg_ref, a_ref, hy_ref, wa_ref, wh_ref,
                  wo_ref, bo_ref, xo_ref, h2_ref, *, tok, d):
    x = x_ref[0]
    sh1, sc1, g1 = mod_ref[0, 0:1, :], mod_ref[0, 1:2, :], mod_ref[0, 2:3, :]
    sh2, sc2 = mod_ref[0, 3:4, :], mod_ref[0, 4:5, :]
    h1 = _mod_norm(x, n1_ref[...], sh1, sc1).astype(BF16)
    gates = jnp.dot(h1, wg_ref[...], preferred_element_type=F32) + bg_ref[...]
    hy = jnp.concatenate([hy_ref[0, j * T_PITCH:j * T_PITCH + SLAB, :] for j in range(tok // SLAB)],
                         axis=0)
    pa = jnp.dot(a_ref[0], wa_ref[...], preferred_element_type=F32)
    ph = jnp.dot(hy.astype(BF16), wh_ref[...], preferred_element_type=F32)
    mixed = jax.nn.sigmoid(gates[:, :d]) * pa + jax.nn.sigmoid(gates[:, d:]) * ph
    y = jnp.dot(mixed.astype(BF16), wo_ref[...], preferred_element_type=F32) + bo_ref[...]
    xn = x + g1 * y
    xo_ref[0] = xn
    h2_ref[0] = _mod_norm(xn, n2_ref[...], sh2, sc2).astype(BF16)


def _merge(x, mod, n1w, n2w, wg, bg, attn_o, hy_o, wa, wh, wo, bo):
    bsz, seq, d = x.shape
    tok = T_TOK
    d_hy = hy_o.shape[2]
    kernel = functools.partial(_merge_kernel, tok=tok, d=d)
    return pl.pallas_call(
        kernel,
        out_shape=(jax.ShapeDtypeStruct((bsz, seq, d), F32), jax.ShapeDtypeStruct((bsz, seq, d), BF16)),
        grid=(bsz, seq // tok),
        in_specs=[
            pl.BlockSpec((1, tok, d), lambda b_, i: (b_, i, 0)),
            pl.BlockSpec((1, N_MOD, d), lambda b_, i: (b_, 0, 0)),
            _const_spec((1, d)), _const_spec((1, d)),
            _const_spec(wg.shape), _const_spec(bg.shape),
            pl.BlockSpec((1, tok, attn_o.shape[2]), lambda b_, i: (b_, i, 0)),
            pl.BlockSpec((1, (tok // SLAB) * T_PITCH, d_hy), lambda b_, i: (b_, i, 0)),
            _const_spec(wa.shape), _const_spec(wh.shape), _const_spec(wo.shape), _const_spec(bo.shape),
        ],
        out_specs=(pl.BlockSpec((1, tok, d), lambda b_, i: (b_, i, 0)),
                   pl.BlockSpec((1, tok, d), lambda b_, i: (b_, i, 0))),
        compiler_params=pltpu.CompilerParams(
            dimension_semantics=("arbitrary", "arbitrary"), vmem_limit_bytes=VMEM_LIMIT),
        name="merge",
    )(x, mod, n1w, n2w, wg, bg, attn_o, hy_o, wa, wh, wo, bo)


def _gelu_tanh(x):
    return 0.5 * x * (1.0 + jnp.tanh(math.sqrt(2.0 / math.pi) * (x + 0.044715 * (x * x * x))))


def _ffn_kernel(hp_ref, hm_ref, hn_ref, x_ref, mod_ref, wu_ref, bu_ref, cw_ref, cb_ref, wd_ref, bd_ref,
                fn_ref, o_ref, pa_buf, pg_buf, act_buf, *, tok, d_ff):
    i = pl.program_id(1)
    nt = pl.num_programs(1)
    hext = jnp.concatenate([hp_ref[0], hm_ref[0], hn_ref[0]], axis=0)
    rows = tok + 2 * HALO_BF16
    row_id = lax.broadcasted_iota(jnp.int32, (rows, 1), 0)
    valid = jnp.logical_and(jnp.logical_or(i > 0, row_id >= HALO_BF16),
                            jnp.logical_or(i < nt - 1, row_id < HALO_BF16 + tok))
    lo = HALO_BF16

    def conv(buf, col0):
        return (cw_ref[0:1, col0:col0 + FF_CHUNK] * buf[lo - 1:lo - 1 + tok, :]
                + cw_ref[1:2, col0:col0 + FF_CHUNK] * buf[lo:lo + tok, :]
                + cw_ref[2:3, col0:col0 + FF_CHUNK] * buf[lo + 1:lo + 1 + tok, :]
                + cb_ref[:, col0:col0 + FF_CHUNK])

    for c in range(d_ff // FF_CHUNK):
        ca = c * FF_CHUNK
        cg = d_ff + c * FF_CHUNK
        pa = jnp.dot(hext, wu_ref[:, ca:ca + FF_CHUNK], preferred_element_type=F32) + bu_ref[:, ca:ca + FF_CHUNK]
        pg = jnp.dot(hext, wu_ref[:, cg:cg + FF_CHUNK], preferred_element_type=F32) + bu_ref[:, cg:cg + FF_CHUNK]
        pa_buf[...] = jnp.where(valid, pa, 0.0)
        pg_buf[...] = jnp.where(valid, pg, 0.0)
        act_buf[:, ca:ca + FF_CHUNK] = (_gelu_tanh(conv(pa_buf, ca)) * conv(pg_buf, cg)).astype(BF16)
    y = jnp.dot(act_buf[...], wd_ref[...], preferred_element_type=F32)
    g2 = mod_ref[0, 5:6, :]
    xn = x_ref[0] + g2 * (y + bd_ref[...])
    o_ref[0] = _rms(xn) * fn_ref[...]


def _ffn(h2, x_new, mod, wu, bu, cw, cb, wd, bd, fnw):
    bsz, seq, d = x_new.shape
    tok = T_TOK
    d_ff = wd.shape[0]
    hb = tok // HALO_BF16
    n_hblk = seq // HALO_BF16
    rows = tok + 2 * HALO_BF16
    kernel = functools.partial(_ffn_kernel, tok=tok, d_ff=d_ff)
    return pl.pallas_call(
        kernel,
        out_shape=jax.ShapeDtypeStruct((bsz, seq, d), F32),
        grid=(bsz, seq // tok),
        in_specs=[
            pl.BlockSpec((1, HALO_BF16, d), lambda b_, i: (b_, jnp.maximum(i * hb - 1, 0), 0)),
            pl.BlockSpec((1, tok, d), lambda b_, i: (b_, i, 0)),
            pl.BlockSpec((1, HALO_BF16, d), lambda b_, i: (b_, jnp.minimum((i + 1) * hb, n_hblk - 1), 0)),
            pl.BlockSpec((1, tok, d), lambda b_, i: (b_, i, 0)),
            pl.BlockSpec((1, N_MOD, d), lambda b_, i: (b_, 0, 0)),
            _const_spec(wu.shape), _const_spec(bu.shape), _const_spec(cw.shape), _const_spec(cb.shape),
            _const_spec(wd.shape), _const_spec(bd.shape), _const_spec(fnw.shape),
        ],
        out_specs=pl.BlockSpec((1, tok, d), lambda b_, i: (b_, i, 0)),
        scratch_shapes=[pltpu.VMEM((rows, FF_CHUNK), F32), pltpu.VMEM((rows, FF_CHUNK), F32),
                        pltpu.VMEM((tok, d_ff), BF16)],
        compiler_params=pltpu.CompilerParams(
            dimension_semantics=("arbitrary", "arbitrary"), vmem_limit_bytes=VMEM_LIMIT),
        name="conv_ffn",
    )(h2, h2, h2, x_new, mod, wu, bu, cw, cb, wd, bd, fnw)


def _rope_tables(seq):
    half = HEAD_DIM // 2
    t = np.arange(seq)
    row = (t // GRID_W).astype(np.float64)
    col = (t % GRID_W).astype(np.float64)
    freqs = ROPE_THETA ** (-np.arange(0, half, 2, dtype=np.float64) / half)
    ang = np.concatenate([row[:, None] * freqs, col[:, None] * freqs], axis=-1)
    cos = np.cos(ang)
    sin = np.sin(ang)
    cos_t = np.concatenate([cos, cos], axis=-1)
    sin_t = np.concatenate([-sin, sin], axis=-1)
    return jnp.asarray(cos_t, F32), jnp.asarray(sin_t, F32)


def _filter_tables(seq, d_hy):
    n = 2 * seq
    j = np.arange(n)
    lag = np.where(j < seq, j, n - j)
    t = lag / max(seq - 1, 1)
    bands = (FILTER_EMB_DIM - 1) // 2
    f = np.linspace(1e-4, bands - 1, bands)
    wpos = 2.0 * math.pi * lag / seq
    z = np.concatenate([t[:, None], np.cos(wpos[:, None] * f), -np.sin(wpos[:, None] * f)], axis=-1)
    z2 = np.zeros((n, LANES), np.float64)
    z2[:, :FILTER_EMB_DIM] = z
    tab = np.zeros((n, SUBLANES), np.float64)
    tab[:, 0] = t
    tab[:, 1] = (j < seq)
    tab[:, 2] = np.logical_or(j == 0, j > seq)
    min_decay = math.log(DECAY_TARGET) / SLOW_DECAY_PCT
    max_decay = math.log(DECAY_TARGET) / FAST_DECAY_PCT
    absdelta = np.abs(np.linspace(min_decay, max_decay, d_hy))[None, :]
    return jnp.asarray(z2, F32), jnp.asarray(tab, F32), jnp.asarray(absdelta, F32)


def _dft_tables(seq):
    n_a = seq // SLAB
    n_s = 2 * n_a
    n = 2 * seq
    n_p = n_a + 1
    half = _round_up(n_p, SUBLANES)
    p = np.arange(half)
    live = (p < n_p).astype(np.float64)
    r = np.arange(SLAB)
    a = np.arange(n_s)
    theta = 2.0 * math.pi * p[None, :, None] * (SLAB * a[None, None, :] + r[:, None, None]) / n
    ff = np.concatenate([np.cos(theta) * live[None, :, None], -np.sin(theta) * live[None, :, None]], axis=1)
    phi = 2.0 * math.pi * np.outer(r, r) / SLAB
    c, s = np.cos(phi), np.sin(phi)
    fa = np.block([[c, s], [-s, c]])
    fai = np.block([[c, -s], [s, c]])
    wgt = np.where((p == 0) | (p == n_a), 1.0, 2.0) * live / n
    th_i = 2.0 * math.pi * (SLAB * a[None, :n_a, None] + r[:, None, None]) * p[None, None, :] / n
    gi = np.concatenate([np.cos(th_i) * wgt[None, None, :], -np.sin(th_i) * wgt[None, None, :]], axis=2)
    to_bf = lambda m: jnp.asarray(m, F32).astype(BF16)
    return to_bf(ff), to_bf(ff[:, :, :n_a]), to_bf(fa), to_bf(fai), to_bf(gi), n_p


def kernel(x, c, ctx, c_ctx, w_mod, b_mod, norm1_w, norm2_w, w_in, b_in, q_norm_w, k_norm_w, hy_conv_w, hy_conv_b, filt_w1, filt_b1, filt_w_inner, filt_b_inner, filt_freq, filt_w_out, hy_skip, w_attn_out, w_hy_out, w_o, b_o, w_up, b_up, ffn_conv_w, ffn_conv_b, w_down, b_down, final_norm_w):
    bsz, seq, d = x.shape
    ctx_len = ctx.shape[1]
    d_hy = hy_skip.shape[2]
    assert w_mod.shape[0] == 1, "single-layer kernel"
    assert seq % T_TOK == 0 and seq % T_Q == 0 and seq % ctx_len == 0 and (2 * seq) % T_FILT == 0
    assert T_TOK % KV_SUB == 0 and ctx_len % KV_SUB == 0
    assert d_hy % LANES == 0 and w_down.shape[1] % FF_CHUNK == 0 and bsz + 1 <= SUBLANES

    cvec = jnp.zeros((SUBLANES, d), F32).at[:bsz].set(c).at[bsz].set(c_ctx)
    mod = _modulation(cvec, w_mod[0], b_mod[0]).reshape(SUBLANES, N_MOD, d)

    qw = N_HEADS * HEAD_DIM
    kw = N_KV_HEADS * HEAD_DIM
    hy0 = qw + 2 * kw
    g0 = hy0 + (HYENA_ORDER + 1) * d_hy
    perm = np.concatenate([np.arange(0, HEAD_DIM, 2), np.arange(1, HEAD_DIM, 2)])
    perm_q = (np.arange(N_HEADS)[:, None] * HEAD_DIM + perm[None, :]).reshape(-1)
    perm_k = qw + (np.arange(N_KV_HEADS)[:, None] * HEAD_DIM + perm[None, :]).reshape(-1)
    cols = np.concatenate([perm_q, perm_k, np.arange(qw + kw, g0)])
    w_main = w_in[0][:, cols].astype(BF16)
    b_main = b_in[0][cols].reshape(1, -1)
    w_kv = w_main[:, qw:hy0]
    b_kv = b_main[:, qw:hy0]
    w_gates = w_in[0][:, g0:].astype(BF16)
    b_gates = b_in[0][g0:].reshape(1, -1)
    qn = (q_norm_w[0][perm] * (math.log2(math.e) * HEAD_DIM ** -0.5)).reshape(1, HEAD_DIM)
    kn = k_norm_w[0][perm].reshape(1, HEAD_DIM)
    cos_t, sin_t = _rope_tables(seq)
    n1w = norm1_w[0].reshape(1, d)
    n2w = norm2_w[0].reshape(1, d)

    q, k_all, v_all, u = _inproj(x, mod, n1w, w_main, b_main, qn, kn, cos_t, sin_t,
                                 hy_conv_w[0], hy_conv_b[0].reshape(1, -1), ctx_len)
    k_all, v_all = _ctx_kv(ctx, mod, n1w, w_kv, b_kv, kn, k_all, v_all, seq)
    attn_o = _attention(q, k_all, v_all)

    z2, tab, absdelta = _filter_tables(seq, d_hy)
    w1p = jnp.zeros((LANES, filt_w1.shape[2]), F32).at[:FILTER_EMB_DIM].set(filt_w1[0])
    w_out = filt_w_out[0].reshape(-1, 2, HYENA_ORDER * d_hy)
    g_time = _filter_time(z2, tab, absdelta, w1p, filt_b1[0].reshape(1, -1), filt_w_inner[0],
                          filt_b_inner[0][:, None, :], filt_freq[0].reshape(1, -1),
                          w_out[:, 0].astype(BF16), w_out[:, 1].astype(BF16))
    ff_full, ff_half, fa, fai, gi, n_p = _dft_tables(seq)
    kspec = _filter_spectrum(g_time, ff_full, fa, n_p)
    skip = hy_skip[0].reshape(HYENA_ORDER, 1, d_hy)
    lanes_per_stream = d_hy // LANES
    z1 = _hyena_conv(u, 2 * lanes_per_stream, u, 0, kspec, 0, skip, ff_half, fa, fai, gi, d_hy)
    hy_o = _hyena_conv(z1, 0, u, lanes_per_stream, kspec, 1, skip, ff_half, fa, fai, gi, d_hy)

    x_new, h2 = _merge(x, mod, n1w, n2w, w_gates, b_gates, attn_o, hy_o,
                       w_attn_out[0].astype(BF16), w_hy_out[0].astype(BF16), w_o[0].astype(BF16),
                       b_o[0].reshape(1, d))
    return _ffn(h2, x_new, mod, w_up[0].astype(BF16), b_up[0].reshape(1, -1), ffn_conv_w[0],
                ffn_conv_b[0].reshape(1, -1), w_down[0].astype(BF16), b_down[0].reshape(1, d),
                final_norm_w.reshape(1, d))
```

```python
import functools
import math

import numpy as np
import jax
import jax.numpy as jnp
from jax import lax
from jax.experimental import pallas as pl
from jax.experimental.pallas import tpu as pltpu

F32 = jnp.float32
BF16 = jnp.bfloat16
HIGHEST = lax.Precision.HIGHEST

GRID_W = 64
N_HEADS = 8
N_KV_HEADS = 2
Q_GROUP = N_HEADS // N_KV_HEADS
HEAD_DIM = 128
ROPE_THETA = 10000.0
RMS_EPS = 1e-6
HYENA_ORDER = 2
FILTER_EMB_DIM = 33
DECAY_TARGET = 1e-2
FAST_DECAY_PCT = 0.3
SLOW_DECAY_PCT = 1.5
N_MOD = 6

LANES = 128
SUBLANES = 8
BF16_ROWS = 16
VMEM_LIMIT = 56 * 1024 * 1024

SLAB = 128
T_PITCH = SLAB + SUBLANES
S_PITCH = 2 * SLAB + SUBLANES

T_TOK = 512
T_FFN = 512
T_Q = 256
KV_SUB = 256
V_ROWS = HEAD_DIM + BF16_ROWS
ATTN_STEPS = 8
ATTN_QSUBS = 2
CROSS_UNROLL = 16
SLAB_UNROLL = 8
T_FILT = 1024
FF_CHUNK = 256
HALO_F32 = SUBLANES
HALO_BF16 = BF16_ROWS


def _round_up(n, m):
    return (n + m - 1) // m * m


def _rms(x):
    return x * lax.rsqrt(jnp.mean(x * x, axis=-1, keepdims=True) + RMS_EPS)


def _mod_norm(x, w, shift, scale):
    return (_rms(x) * w) * (1.0 + scale) + shift


def _const_spec(shape):
    n = len(shape)
    return pl.BlockSpec(shape, lambda *_: (0,) * n, pipeline_mode=pl.Buffered(1))


def _modulation_kernel(c_ref, w_ref, b_ref, o_ref):
    c = c_ref[...]
    s = c * jax.nn.sigmoid(c)
    o_ref[...] = jnp.dot(s, w_ref[...], precision=HIGHEST, preferred_element_type=F32) + b_ref[...]


def _modulation(cvec, w_mod, b_mod):
    rows, d = cvec.shape
    width = w_mod.shape[1]
    blk = d
    return pl.pallas_call(
        _modulation_kernel,
        out_shape=jax.ShapeDtypeStruct((rows, width), F32),
        grid=(width // blk,),
        in_specs=[pl.BlockSpec((rows, d), lambda j: (0, 0)),
                  pl.BlockSpec((d, blk), lambda j: (0, j)),
                  pl.BlockSpec((1, blk), lambda j: (0, j))],
        out_specs=pl.BlockSpec((rows, blk), lambda j: (0, j)),
        compiler_params=pltpu.CompilerParams(dimension_semantics=("arbitrary",)),
        name="modulation",
    )(cvec, w_mod, b_mod.reshape(1, width))


def _head_norm(p, w):
    return _rms(p) * w


def _values_block(v):
    ones = jnp.ones((V_ROWS - HEAD_DIM, v.shape[0]), F32)
    return jnp.concatenate([v.T, ones], axis=0).astype(BF16)


def _rope(p, cos, sin):
    return p * cos + pltpu.roll(p, HEAD_DIM // 2, axis=1) * sin


def _inproj_kernel(xp_ref, xm_ref, xn_ref, mod_ref, n1_ref, w_ref, b_ref, qn_ref, kn_ref,
                   cos_ref, sin_ref, cw_ref, cb_ref,
                   q_ref, k_ref, v_ref, u_ref, pbuf, *, tok, d_hy3):
    i = pl.program_id(1)
    nt = pl.num_programs(1)
    shift = mod_ref[0, 0:1, :]
    scale = mod_ref[0, 1:2, :]
    n1 = n1_ref[...]
    hm = _mod_norm(xm_ref[0], n1, shift, scale).astype(BF16)
    halo = jnp.concatenate([xp_ref[0], xn_ref[0]], axis=0)
    hh = _mod_norm(halo, n1, shift, scale).astype(BF16)

    qw = N_HEADS * HEAD_DIM
    kw = N_KV_HEADS * HEAD_DIM
    hy0 = qw + 2 * kw
    cos = cos_ref[...]
    sin = sin_ref[...]
    qn = qn_ref[...]
    kn = kn_ref[...]

    p = jnp.dot(hm, w_ref[...], preferred_element_type=F32) + b_ref[...]
    for h in range(N_HEADS):
        ph = _rope(_head_norm(p[:, h * HEAD_DIM:(h + 1) * HEAD_DIM], qn), cos, sin)
        for j in range(tok // T_Q):
            q_ref[0, h, j] = ph[j * T_Q:(j + 1) * T_Q].T.astype(BF16)
    for h in range(N_KV_HEADS):
        ph = p[:, qw + h * HEAD_DIM: qw + (h + 1) * HEAD_DIM]
        k_ref[0, h] = _rope(_head_norm(ph, kn), cos, sin).astype(BF16)
        pv = p[:, qw + kw + h * HEAD_DIM: qw + kw + (h + 1) * HEAD_DIM]
        for j in range(tok // KV_SUB):
            v_ref[0, h, j] = _values_block(pv[j * KV_SUB:(j + 1) * KV_SUB])

    ph = jnp.dot(hh, w_ref[:, hy0:], preferred_element_type=F32) + b_ref[:, hy0:]
    top = jnp.where(i == 0, 0.0, ph[:HALO_F32])
    bot = jnp.where(i == nt - 1, 0.0, ph[HALO_F32:])
    pbuf[0:HALO_F32, :] = top
    pbuf[HALO_F32:HALO_F32 + tok, :] = p[:, hy0:]
    pbuf[HALO_F32 + tok:, :] = bot
    u = (cw_ref[0:1, :] * pbuf[HALO_F32 - 1:HALO_F32 - 1 + tok, :]
         + cw_ref[1:2, :] * pbuf[HALO_F32:HALO_F32 + tok, :]
         + cw_ref[2:3, :] * pbuf[HALO_F32 + 1:HALO_F32 + 1 + tok, :]
         + cb_ref[...])
    pad = jnp.zeros((T_PITCH - SLAB, d_hy3), F32)
    for j in range(tok // SLAB):
        u_ref[0, j * T_PITCH:j * T_PITCH + SLAB, :] = u[j * SLAB:(j + 1) * SLAB]
        u_ref[0, j * T_PITCH + SLAB:(j + 1) * T_PITCH, :] = pad


def _inproj(x, mod, n1w, w, b, qn, kn, cos_t, sin_t, cw, cb, ctx_len):
    bsz, seq, d = x.shape
    tok = T_TOK
    nt = seq // tok
    width = w.shape[1]
    qw = N_HEADS * HEAD_DIM
    kw = N_KV_HEADS * HEAD_DIM
    d_hy3 = width - qw - 2 * kw
    hb = tok // HALO_F32
    n_hblk = seq // HALO_F32
    kernel = functools.partial(_inproj_kernel, tok=tok, d_hy3=d_hy3)
    rows_p = (seq // SLAB) * T_PITCH
    out_shape = (
        jax.ShapeDtypeStruct((bsz, N_HEADS, seq // T_Q, HEAD_DIM, T_Q), BF16),
        jax.ShapeDtypeStruct((bsz, N_KV_HEADS, seq + ctx_len, HEAD_DIM), BF16),
        jax.ShapeDtypeStruct((bsz, N_KV_HEADS, (seq + ctx_len) // KV_SUB, V_ROWS, KV_SUB), BF16),
        jax.ShapeDtypeStruct((bsz, rows_p, d_hy3), F32),
    )
    return pl.pallas_call(
        kernel,
        out_shape=out_shape,
        grid=(bsz, nt),
        in_specs=[
            pl.BlockSpec((1, HALO_F32, d), lambda b_, i: (b_, jnp.maximum(i * hb - 1, 0), 0)),
            pl.BlockSpec((1, tok, d), lambda b_, i: (b_, i, 0)),
            pl.BlockSpec((1, HALO_F32, d), lambda b_, i: (b_, jnp.minimum((i + 1) * hb, n_hblk - 1), 0)),
            pl.BlockSpec((1, N_MOD, d), lambda b_, i: (b_, 0, 0)),
            _const_spec((1, d)),
            _const_spec((d, width)),
            _const_spec((1, width)),
            _const_spec((1, HEAD_DIM)),
            _const_spec((1, HEAD_DIM)),
            pl.BlockSpec((tok, HEAD_DIM), lambda b_, i: (i, 0)),
            pl.BlockSpec((tok, HEAD_DIM), lambda b_, i: (i, 0)),
            _const_spec((3, d_hy3)),
            _const_spec((1, d_hy3)),
        ],
        out_specs=(
            pl.BlockSpec((1, N_HEADS, tok // T_Q, HEAD_DIM, T_Q), lambda b_, i: (b_, 0, i, 0, 0)),
            pl.BlockSpec((1, N_KV_HEADS, tok, HEAD_DIM), lambda b_, i: (b_, 0, i, 0)),
            pl.BlockSpec((1, N_KV_HEADS, tok // KV_SUB, V_ROWS, KV_SUB), lambda b_, i: (b_, 0, i, 0, 0)),
            pl.BlockSpec((1, (tok // SLAB) * T_PITCH, d_hy3), lambda b_, i: (b_, i, 0)),
        ),
        scratch_shapes=[pltpu.VMEM((tok + 2 * HALO_F32, d_hy3), F32)],
        compiler_params=pltpu.CompilerParams(
            dimension_semantics=("arbitrary", "arbitrary"), vmem_limit_bytes=VMEM_LIMIT),
        name="inproj",
    )(x, x, x, mod, n1w, w, b, qn, kn, cos_t, sin_t, cw, cb)


def _ctx_kv_kernel(c_ref, mod_ref, n1_ref, w_ref, b_ref, kn_ref, kin_ref, vin_ref, k_ref, v_ref):
    del kin_ref, vin_ref
    h = _mod_norm(c_ref[0], n1_ref[...], mod_ref[0, 0:1, :], mod_ref[0, 1:2, :]).astype(BF16)
    p = jnp.dot(h, w_ref[...], preferred_element_type=F32) + b_ref[...]
    kw = N_KV_HEADS * HEAD_DIM
    kn = kn_ref[...]
    for hd in range(N_KV_HEADS):
        k_ref[0, hd] = _head_norm(p[:, hd * HEAD_DIM:(hd + 1) * HEAD_DIM], kn).astype(BF16)
        pv = p[:, kw + hd * HEAD_DIM: kw + (hd + 1) * HEAD_DIM]
        for j in range(pv.shape[0] // KV_SUB):
            v_ref[0, hd, j] = _values_block(pv[j * KV_SUB:(j + 1) * KV_SUB])


def _ctx_kv(ctx, mod, n1w, w, b, kn, k_all, v_all, seq):
    bsz, ctx_len, d = ctx.shape
    ctx_row = bsz
    blk = seq // ctx_len
    any_spec = pl.BlockSpec(memory_space=pl.ANY)
    return pl.pallas_call(
        _ctx_kv_kernel,
        out_shape=(jax.ShapeDtypeStruct(k_all.shape, k_all.dtype),
                   jax.ShapeDtypeStruct(v_all.shape, v_all.dtype)),
        grid=(bsz,),
        in_specs=[
            pl.BlockSpec((1, ctx_len, d), lambda b_: (b_, 0, 0)),
            pl.BlockSpec((1, N_MOD, d), lambda b_: (ctx_row, 0, 0)),
            _const_spec((1, d)),
            _const_spec(w.shape),
            _const_spec(b.shape),
            _const_spec((1, HEAD_DIM)),
            any_spec, any_spec,
        ],
        out_specs=(
            pl.BlockSpec((1, N_KV_HEADS, ctx_len, HEAD_DIM), lambda b_: (b_, 0, blk, 0)),
            pl.BlockSpec((1, N_KV_HEADS, ctx_len // KV_SUB, V_ROWS, KV_SUB),
                         lambda b_: (b_, 0, seq // ctx_len, 0, 0)),
        ),
        input_output_aliases={6: 0, 7: 1},
        compiler_params=pltpu.CompilerParams(dimension_semantics=("arbitrary",)),
        name="ctx_kv",
    )(ctx, mod, n1w, w, b, kn, k_all, v_all)


def _attn_kernel(q_ref, k_ref, v_ref, o_ref, s_a, s_b, p_a, p_b, acc_ref, *, n_sub):
    tq = q_ref.shape[4]
    chains = [(g, j) for g in range(Q_GROUP) for j in range(q_ref.shape[2])]

    def scores(c, s_out):
        start = c * KV_SUB
        if not isinstance(c, int):
            start = pl.multiple_of(start, KV_SUB)
        kc = k_ref[0, 0, pl.ds(start, KV_SUB), :]
        for ch, (g, j) in enumerate(chains):
            s_out[ch] = jnp.dot(kc, q_ref[0, g, j], preferred_element_type=F32)

    def softmax(s_in, p_out, stats):
        new_stats, alphas = [], []
        for ch in range(len(chains)):
            s = s_in[ch]
            m_new = jnp.maximum(stats[ch], jnp.max(s, axis=0, keepdims=True))
            alphas.append(jnp.exp2(stats[ch] - m_new))
            p_out[ch] = jnp.exp2(s - m_new).astype(BF16)
            new_stats.append(m_new)
        return tuple(new_stats), tuple(alphas)

    def values(c, p_in, alphas):
        vt = v_ref[0, 0, c]
        for ch in range(len(chains)):
            acc_ref[ch] = alphas[ch] * acc_ref[ch] + jnp.dot(vt, p_in[ch], preferred_element_type=F32)

    assert n_sub % 2 == 1 and n_sub >= 3
    acc_ref[...] = jnp.zeros(acc_ref.shape, F32)
    stats = (jnp.full((1, tq), -jnp.inf, F32),) * len(chains)
    scores(0, s_a)
    scores(1, s_b)
    stats, alphas = softmax(s_a, p_a, stats)

    def group(t, carry):
        stats, alphas = carry
        for u in range(ATTN_STEPS):
            c = ATTN_STEPS * t + 1 + u
            s_next, s_cur, p_cur, p_prev = (s_a, s_b, p_b, p_a) if u % 2 == 0 else (s_b, s_a, p_a, p_b)
            scores(jnp.minimum(c + 1, n_sub - 1), s_next)
            stats, alphas_new = softmax(s_cur, p_cur, stats)
            values(c - 1, p_prev, alphas)
            alphas = alphas_new
        return stats, alphas

    assert (n_sub - 1) % ATTN_STEPS == 0 and ATTN_STEPS % 2 == 0
    stats, alphas = lax.fori_loop(0, (n_sub - 1) // ATTN_STEPS, group, (stats, alphas))
    values(n_sub - 1, p_a, alphas)
    for ch, (g, j) in enumerate(chains):
        out = acc_ref[ch, :HEAD_DIM, :] / acc_ref[ch, HEAD_DIM:HEAD_DIM + 1, :]
        o_ref[0, j * tq:(j + 1) * tq, g * HEAD_DIM:(g + 1) * HEAD_DIM] = out.T.astype(o_ref.dtype)


def _attention(q, k_all, v_all):
    bsz, _, n_qsub, _, tq = q.shape
    n_keys = k_all.shape[2]
    n_sub = v_all.shape[2]
    qs = ATTN_QSUBS
    n_chain = Q_GROUP * qs
    kernel = functools.partial(_attn_kernel, n_sub=n_sub)
    return pl.pallas_call(
        kernel,
        out_shape=jax.ShapeDtypeStruct((bsz, n_qsub * tq, N_HEADS * HEAD_DIM), BF16),
        grid=(bsz, N_KV_HEADS, n_qsub // qs),
        in_specs=[
            pl.BlockSpec((1, Q_GROUP, qs, HEAD_DIM, tq), lambda b_, h, i: (b_, h, i, 0, 0)),
            pl.BlockSpec((1, 1, n_keys, HEAD_DIM), lambda b_, h, i: (b_, h, 0, 0)),
            pl.BlockSpec((1, 1, n_sub, V_ROWS, KV_SUB), lambda b_, h, i: (b_, h, 0, 0, 0)),
        ],
        out_specs=pl.BlockSpec((1, qs * tq, Q_GROUP * HEAD_DIM), lambda b_, h, i: (b_, i, h)),
        scratch_shapes=[
            pltpu.VMEM((n_chain, KV_SUB, tq), F32), pltpu.VMEM((n_chain, KV_SUB, tq), F32),
            pltpu.VMEM((n_chain, KV_SUB, tq), BF16), pltpu.VMEM((n_chain, KV_SUB, tq), BF16),
            pltpu.VMEM((n_chain, V_ROWS, tq), F32),
        ],
        compiler_params=pltpu.CompilerParams(
            dimension_semantics=("arbitrary", "arbitrary", "arbitrary"), vmem_limit_bytes=VMEM_LIMIT),
        name="attention",
    )(q, k_all, v_all)


def _filter_kernel(z_ref, tab_ref, dl_ref, w1_ref, b1_ref, wi_ref, bi_ref, fr_ref, wf_ref, wb_ref,
                   g_ref, *, rows, d_hy, n_inner):
    half = rows // 2
    fr = fr_ref[...]
    pre = (jnp.dot(z_ref[:half, :], w1_ref[0], precision=HIGHEST, preferred_element_type=F32)
           + jnp.dot(z_ref[half:, :], w1_ref[1], precision=HIGHEST, preferred_element_type=F32))
    h = jnp.sin(fr * (pre + b1_ref[...]))
    for j in range(n_inner):
        h = jnp.sin(fr * (jnp.dot(h, wi_ref[j], precision=HIGHEST,
                                  preferred_element_type=F32) + bi_ref[j]))
    hb = h.astype(BF16)
    pad = jnp.zeros((T_PITCH - SLAB, d_hy), F32)
    for part in range(2):
        pf = jnp.dot(hb, wf_ref[part], preferred_element_type=F32)
        pb = jnp.dot(hb, wb_ref[part], preferred_element_type=F32)
        r0 = part * half
        t = tab_ref[r0:r0 + half, 0:1]
        mf = tab_ref[r0:r0 + half, 1:2]
        mb = tab_ref[r0:r0 + half, 2:3]
        decay = jnp.exp(-t * dl_ref[...])
        for o in range(HYENA_ORDER):
            g = (mf * pf[:, o * d_hy:(o + 1) * d_hy] + mb * pb[:, o * d_hy:(o + 1) * d_hy]) * decay
            for j in range(half // SLAB):
                row = (r0 // SLAB + j) * T_PITCH
                g_ref[o, row:row + SLAB, :] = g[j * SLAB:(j + 1) * SLAB]
                g_ref[o, row + SLAB:row + T_PITCH, :] = pad


def _filter_time(z2, tab, absdelta, w1p, b1, wi, bi, freq, wf, wb):
    n_rows = z2.shape[0]
    d_hy = absdelta.shape[1]
    rows = T_FILT
    n_inner = wi.shape[0]
    kernel = functools.partial(_filter_kernel, rows=rows, d_hy=d_hy, n_inner=n_inner)
    return pl.pallas_call(
        kernel,
        out_shape=jax.ShapeDtypeStruct((HYENA_ORDER, (n_rows // SLAB) * T_PITCH, d_hy), F32),
        grid=(n_rows // rows,),
        in_specs=[
            pl.BlockSpec((rows, z2.shape[1]), lambda i: (i, 0)),
            pl.BlockSpec((rows, tab.shape[1]), lambda i: (i, 0)),
            _const_spec(absdelta.shape), _const_spec(w1p.shape), _const_spec(b1.shape),
            _const_spec(wi.shape), _const_spec(bi.shape), _const_spec(freq.shape),
            _const_spec(wf.shape), _const_spec(wb.shape),
        ],
        out_specs=pl.BlockSpec((HYENA_ORDER, (rows // SLAB) * T_PITCH, d_hy), lambda i: (0, i, 0)),
        compiler_params=pltpu.CompilerParams(
            dimension_semantics=("arbitrary",), vmem_limit_bytes=VMEM_LIMIT),
        name="hyena_filter",
    )(z2, tab, absdelta, w1p, b1, wi, bi, freq, wf, wb)


def _loop(n, body, unroll):
    def group(t, carry):
        for j in range(unroll):
            body(t * unroll + j)
        return carry

    lax.fori_loop(0, n // unroll, group, 0)
    for i in range(n - n % unroll, n):
        body(i)


def _forward_cross_slab(src_ref, n_in, ff_ref, s_ref, half):
    def body(r):
        g = src_ref[pl.ds(r, n_in, stride=T_PITCH), :]
        a = jnp.dot(ff_ref[r], g.astype(BF16), preferred_element_type=F32)
        s_ref[pl.ds(r, half, stride=S_PITCH), :] = a[:half]
        s_ref[pl.ds(SLAB + r, half, stride=S_PITCH), :] = a[half:]

    _loop(SLAB, body, CROSS_UNROLL)


def _filter_spec_kernel(g_ref, ff_ref, fa_ref, k_ref, s_ref, *, n_in, n_p, half):
    _forward_cross_slab(g_ref.at[0], n_in, ff_ref, s_ref, half)
    fa = fa_ref[...]

    def slab(p):
        start = p * S_PITCH
        if not isinstance(p, int):
            start = pl.multiple_of(start, SUBLANES)
        a = s_ref[pl.ds(start, 2 * SLAB), :]
        k_ref[0, p] = jnp.dot(fa, a.astype(BF16), preferred_element_type=F32).astype(k_ref.dtype)

    _loop(n_p, slab, SLAB_UNROLL)


def _filter_spectrum(g, ff_full, fa, n_p):
    n_ord, rows_p, d_hy = g.shape
    n_in = rows_p // T_PITCH
    half = ff_full.shape[1] // 2
    kernel = functools.partial(_filter_spec_kernel, n_in=n_in, n_p=n_p, half=half)
    return pl.pallas_call(
        kernel,
        out_shape=jax.ShapeDtypeStruct((n_ord, n_p, 2 * SLAB, d_hy), BF16),
        grid=(n_ord, d_hy // LANES),
        in_specs=[
            pl.BlockSpec((1, rows_p, LANES), lambda o, c: (o, 0, c)),
            _const_spec(ff_full.shape),
            _const_spec(fa.shape),
        ],
        out_specs=pl.BlockSpec((1, n_p, 2 * SLAB, LANES), lambda o, c: (o, 0, 0, c)),
        scratch_shapes=[pltpu.VMEM((half * S_PITCH, LANES), F32)],
        compiler_params=pltpu.CompilerParams(
            dimension_semantics=("arbitrary", "arbitrary"), vmem_limit_bytes=VMEM_LIMIT),
        name="hyena_filter_spectrum",
    )(g, ff_full, fa)


def _conv_kernel(z_ref, x_ref, k_ref, sk_ref, ff_ref, fa_ref, fai_ref, gi_ref, o_ref, s_ref,
                 *, n_a, n_p, half):
    _forward_cross_slab(z_ref.at[0], n_a, ff_ref, s_ref, half)
    fa = fa_ref[...]
    fai = fai_ref[...]

    def slab(p):
        start = p * S_PITCH
        if not isinstance(p, int):
            start = pl.multiple_of(start, SUBLANES)
        a = s_ref[pl.ds(start, 2 * SLAB), :]
        x = jnp.dot(fa, a.astype(BF16), preferred_element_type=F32)
        kk = k_ref[0, p].astype(F32)
        xr, xi = x[:SLAB], x[SLAB:]
        kr, ki = kk[:SLAB], kk[SLAB:]
        y = jnp.concatenate([xr * kr - xi * ki, xr * ki + xi * kr], axis=0)
        s_ref[pl.ds(start, 2 * SLAB), :] = jnp.dot(fai, y.astype(BF16), preferred_element_type=F32)

    _loop(n_p, slab, SLAB_UNROLL)
    skip = sk_ref[0]

    def inverse(r):
        br = s_ref[pl.ds(r, half, stride=S_PITCH), :]
        bi = s_ref[pl.ds(SLAB + r, half, stride=S_PITCH), :]
        bb = jnp.concatenate([br, bi], axis=0).astype(BF16)
        y = jnp.dot(gi_ref[r], bb, preferred_element_type=F32)
        z = z_ref[0, pl.ds(r, n_a, stride=T_PITCH), :]
        gate = x_ref[0, pl.ds(r, n_a, stride=T_PITCH), :]
        o_ref[0, pl.ds(r, n_a, stride=T_PITCH), :] = gate * (y + skip * z)

    _loop(SLAB, inverse, CROSS_UNROLL)
    pad = jnp.zeros((T_PITCH - SLAB, LANES), F32)
    for a in range(n_a):
        o_ref[0, a * T_PITCH + SLAB:(a + 1) * T_PITCH, :] = pad


def _hyena_conv(zin, z_blk0, gate, g_blk0, kspec, order, skip, ff, fa, fai, gi, d_hy):
    bsz, rows_p, _ = zin.shape
    n_a = rows_p // T_PITCH
    n_p = kspec.shape[1]
    half = ff.shape[1] // 2
    kernel = functools.partial(_conv_kernel, n_a=n_a, n_p=n_p, half=half)
    return pl.pallas_call(
        kernel,
        out_shape=jax.ShapeDtypeStruct((bsz, rows_p, d_hy), F32),
        grid=(d_hy // LANES, bsz),
        in_specs=[
            pl.BlockSpec((1, rows_p, LANES), lambda c, b_: (b_, 0, z_blk0 + c)),
            pl.BlockSpec((1, rows_p, LANES), lambda c, b_: (b_, 0, g_blk0 + c),
                         pipeline_mode=pl.Buffered(1)),
            pl.BlockSpec((1, n_p, 2 * SLAB, LANES), lambda c, b_: (order, 0, 0, c),
                         pipeline_mode=pl.Buffered(1)),
            pl.BlockSpec((1, 1, LANES), lambda c, b_: (order, 0, c)),
            _const_spec(ff.shape), _const_spec(fa.shape), _const_spec(fai.shape),
            _const_spec(gi.shape),
        ],
        out_specs=pl.BlockSpec((1, rows_p, LANES), lambda c, b_: (b_, 0, c)),
        scratch_shapes=[pltpu.VMEM((half * S_PITCH, LANES), F32)],
        compiler_params=pltpu.CompilerParams(
            dimension_semantics=("arbitrary", "arbitrary"), vmem_limit_bytes=VMEM_LIMIT),
        name=f"hyena_conv{order}",
    )(zin, gate, kspec, skip, ff, fa, fai, gi)


def _merge_kernel(x_ref, mod_ref, n1_ref, n2_ref, wg_ref, bg_ref, a_ref, hy_ref, wa_ref, wh_ref,
                  wo_ref, bo_ref, xo_ref, h2_ref, *, tok, d):
    x = x_ref[0]
    sh1, sc1, g1 = mod_ref[0, 0:1, :], mod_ref[0, 1:2, :], mod_ref[0, 2:3, :]
    sh2, sc2 = mod_ref[0, 3:4, :], mod_ref[0, 4:5, :]
    h1 = _mod_norm(x, n1_ref[...], sh1, sc1).astype(BF16)
    gates = jnp.dot(h1, wg_ref[...], preferred_element_type=F32) + bg_ref[...]
    hy = jnp.concatenate([hy_ref[0, j * T_PITCH:j * T_PITCH + SLAB, :] for j in range(tok // SLAB)],
                         axis=0)
    pa = jnp.dot(a_ref[0], wa_ref[...], preferred_element_type=F32)
    ph = jnp.dot(hy.astype(BF16), wh_ref[...], preferred_element_type=F32)
    mixed = jax.nn.sigmoid(gates[:, :d]) * pa + jax.nn.sigmoid(gates[:, d:]) * ph
    y = jnp.dot(mixed.astype(BF16), wo_ref[...], preferred_element_type=F32) + bo_ref[...]
    xn = x + g1 * y
    xo_ref[0] = xn
    h2_ref[0] = _mod_norm(xn, n2_ref[...], sh2, sc2).astype(BF16)


def _merge(x, mod, n1w, n2w, wg, bg, attn_o, hy_o, wa, wh, wo, bo):
    bsz, seq, d = x.shape
    tok = T_TOK
    d_hy = hy_o.shape[2]
    kernel = functools.partial(_merge_kernel, tok=tok, d=d)
    return pl.pallas_call(
        kernel,
        out_shape=(jax.ShapeDtypeStruct((bsz, seq, d), F32), jax.ShapeDtypeStruct((bsz, seq, d), BF16)),
        grid=(bsz, seq // tok),
        in_specs=[
            pl.BlockSpec((1, tok, d), lambda b_, i: (b_, i, 0)),
            pl.BlockSpec((1, N_MOD, d), lambda b_, i: (b_, 0, 0)),
            _const_spec((1, d)), _const_spec((1, d)),
            _const_spec(wg.shape), _const_spec(bg.shape),
            pl.BlockSpec((1, tok, attn_o.shape[2]), lambda b_, i: (b_, i, 0)),
            pl.BlockSpec((1, (tok // SLAB) * T_PITCH, d_hy), lambda b_, i: (b_, i, 0)),
            _const_spec(wa.shape), _const_spec(wh.shape), _const_spec(wo.shape), _const_spec(bo.shape),
        ],
        out_specs=(pl.BlockSpec((1, tok, d), lambda b_, i: (b_, i, 0)),
                   pl.BlockSpec((1, tok, d), lambda b_, i: (b_, i, 0))),
        compiler_params=pltpu.CompilerParams(
            dimension_semantics=("arbitrary", "arbitrary"), vmem_limit_bytes=VMEM_LIMIT),
        name="merge",
    )(x, mod, n1w, n2w, wg, bg, attn_o, hy_o, wa, wh, wo, bo)


def _gelu_tanh(x):
    return 0.5 * x * (1.0 + jnp.tanh(math.sqrt(2.0 / math.pi) * (x + 0.044715 * (x * x * x))))


def _ffn_kernel(hp_ref, hm_ref, hn_ref, x_ref, mod_ref, wu_ref, bu_ref, cw_ref, cb_ref, wd_ref, bd_ref,
                fn_ref, o_ref, pa_buf, pg_buf, act_buf, *, tok, d_ff):
    i = pl.program_id(1)
    nt = pl.num_programs(1)
    hext = jnp.concatenate([hp_ref[0], hm_ref[0], hn_ref[0]], axis=0)
    lo = HALO_BF16

    def fill(buf, col0):
        cs = slice(col0, col0 + FF_CHUNK)
        p = jnp.dot(hext, wu_ref[:, cs], preferred_element_type=F32)
        nb = -bu_ref[:, cs]
        buf[0:lo, :] = jnp.where(i == 0, nb, p[0:lo])
        buf[lo:lo + tok, :] = p[lo:lo + tok]
        buf[lo + tok:, :] = jnp.where(i == nt - 1, nb, p[lo + tok:])

    def conv(buf, col0):
        cs = slice(col0, col0 + FF_CHUNK)
        w0, w1, w2 = cw_ref[0:1, cs], cw_ref[1:2, cs], cw_ref[2:3, cs]
        const = (w0 + w1 + w2) * bu_ref[:, cs] + cb_ref[:, cs]
        return (w0 * buf[lo - 1:lo - 1 + tok, :] + w1 * buf[lo:lo + tok, :]
                + w2 * buf[lo + 1:lo + 1 + tok, :] + const)

    for c in range(d_ff // FF_CHUNK):
        ca = c * FF_CHUNK
        cg = d_ff + c * FF_CHUNK
        fill(pa_buf, ca)
        fill(pg_buf, cg)
        act_buf[:, ca:ca + FF_CHUNK] = (_gelu_tanh(conv(pa_buf, ca)) * conv(pg_buf, cg)).astype(BF16)
    y = jnp.dot(act_buf[...], wd_ref[...], preferred_element_type=F32)
    g2 = mod_ref[0, 5:6, :]
    xn = x_ref[0] + g2 * (y + bd_ref[...])
    o_ref[0] = _rms(xn) * fn_ref[...]


def _ffn(h2, x_new, mod, wu, bu, cw, cb, wd, bd, fnw):
    bsz, seq, d = x_new.shape
    tok = T_FFN
    d_ff = wd.shape[0]
    hb = tok // HALO_BF16
    n_hblk = seq // HALO_BF16
    rows = tok + 2 * HALO_BF16
    kernel = functools.partial(_ffn_kernel, tok=tok, d_ff=d_ff)
    return pl.pallas_call(
        kernel,
        out_shape=jax.ShapeDtypeStruct((bsz, seq, d), F32),
        grid=(bsz, seq // tok),
        in_specs=[
            pl.BlockSpec((1, HALO_BF16, d), lambda b_, i: (b_, jnp.maximum(i * hb - 1, 0), 0)),
            pl.BlockSpec((1, tok, d), lambda b_, i: (b_, i, 0)),
            pl.BlockSpec((1, HALO_BF16, d), lambda b_, i: (b_, jnp.minimum((i + 1) * hb, n_hblk - 1), 0)),
            pl.BlockSpec((1, tok, d), lambda b_, i: (b_, i, 0)),
            pl.BlockSpec((1, N_MOD, d), lambda b_, i: (b_, 0, 0)),
            _const_spec(wu.shape), _const_spec(bu.shape), _const_spec(cw.shape), _const_spec(cb.shape),
            _const_spec(wd.shape), _const_spec(bd.shape), _const_spec(fnw.shape),
        ],
        out_specs=pl.BlockSpec((1, tok, d), lambda b_, i: (b_, i, 0)),
        scratch_shapes=[pltpu.VMEM((rows, FF_CHUNK), F32), pltpu.VMEM((rows, FF_CHUNK), F32),
                        pltpu.VMEM((tok, d_ff), BF16)],
        compiler_params=pltpu.CompilerParams(
            dimension_semantics=("arbitrary", "arbitrary"), vmem_limit_bytes=VMEM_LIMIT),
        name="conv_ffn",
    )(h2, h2, h2, x_new, mod, wu, bu, cw, cb, wd, bd, fnw)


def _rope_tables(seq):
    half = HEAD_DIM // 2
    t = np.arange(seq)
    row = (t // GRID_W).astype(np.float64)
    col = (t % GRID_W).astype(np.float64)
    freqs = ROPE_THETA ** (-np.arange(0, half, 2, dtype=np.float64) / half)
    ang = np.concatenate([row[:, None] * freqs, col[:, None] * freqs], axis=-1)
    cos = np.cos(ang)
    sin = np.sin(ang)
    cos_t = np.concatenate([cos, cos], axis=-1)
    sin_t = np.concatenate([-sin, sin], axis=-1)
    return jnp.asarray(cos_t, F32), jnp.asarray(sin_t, F32)


def _filter_tables(seq, d_hy):
    n = 2 * seq
    j = np.arange(n)
    lag = np.where(j < seq, j, n - j)
    t = lag / max(seq - 1, 1)
    bands = (FILTER_EMB_DIM - 1) // 2
    f = np.linspace(1e-4, bands - 1, bands)
    wpos = 2.0 * math.pi * lag / seq
    z = np.concatenate([t[:, None], np.cos(wpos[:, None] * f), -np.sin(wpos[:, None] * f)], axis=-1)
    z2 = np.zeros((n, LANES), np.float64)
    z2[:, :FILTER_EMB_DIM] = z
    tab = np.zeros((n, SUBLANES), np.float64)
    tab[:, 0] = t
    tab[:, 1] = (j < seq)
    tab[:, 2] = np.logical_or(j == 0, j > seq)
    min_decay = math.log(DECAY_TARGET) / SLOW_DECAY_PCT
    max_decay = math.log(DECAY_TARGET) / FAST_DECAY_PCT
    absdelta = np.abs(np.linspace(min_decay, max_decay, d_hy))[None, :]
    return jnp.asarray(z2, F32), jnp.asarray(tab, F32), jnp.asarray(absdelta, F32)


def _dft_tables(seq):
    n_a = seq // SLAB
    n_s = 2 * n_a
    n = 2 * seq
    n_p = n_a + 1
    half = _round_up(n_p, SUBLANES)
    p = np.arange(half)
    live = (p < n_p).astype(np.float64)
    r = np.arange(SLAB)
    a = np.arange(n_s)
    theta = 2.0 * math.pi * p[None, :, None] * (SLAB * a[None, None, :] + r[:, None, None]) / n
    ff = np.concatenate([np.cos(theta) * live[None, :, None], -np.sin(theta) * live[None, :, None]], axis=1)
    phi = 2.0 * math.pi * np.outer(r, r) / SLAB
    c, s = np.cos(phi), np.sin(phi)
    fa = np.block([[c, s], [-s, c]])
    fai = np.block([[c, -s], [s, c]])
    wgt = np.where((p == 0) | (p == n_a), 1.0, 2.0) * live / n
    th_i = 2.0 * math.pi * (SLAB * a[None, :n_a, None] + r[:, None, None]) * p[None, None, :] / n
    gi = np.concatenate([np.cos(th_i) * wgt[None, None, :], -np.sin(th_i) * wgt[None, None, :]], axis=2)
    to_bf = lambda m: jnp.asarray(m, F32).astype(BF16)
    return to_bf(ff), to_bf(ff[:, :, :n_a]), to_bf(fa), to_bf(fai), to_bf(gi), n_p


def kernel(x, c, ctx, c_ctx, w_mod, b_mod, norm1_w, norm2_w, w_in, b_in, q_norm_w, k_norm_w, hy_conv_w, hy_conv_b, filt_w1, filt_b1, filt_w_inner, filt_b_inner, filt_freq, filt_w_out, hy_skip, w_attn_out, w_hy_out, w_o, b_o, w_up, b_up, ffn_conv_w, ffn_conv_b, w_down, b_down, final_norm_w):
    bsz, seq, d = x.shape
    ctx_len = ctx.shape[1]
    d_hy = hy_skip.shape[2]
    assert w_mod.shape[0] == 1, "single-layer kernel"
    assert seq % T_TOK == 0 and seq % T_Q == 0 and seq % ctx_len == 0 and (2 * seq) % T_FILT == 0
    assert T_TOK % KV_SUB == 0 and ctx_len % KV_SUB == 0 and seq % T_FFN == 0
    assert d_hy % LANES == 0 and w_down.shape[1] % FF_CHUNK == 0 and bsz + 1 <= SUBLANES

    cvec = jnp.zeros((SUBLANES, d), F32).at[:bsz].set(c).at[bsz].set(c_ctx)
    mod = _modulation(cvec, w_mod[0], b_mod[0]).reshape(SUBLANES, N_MOD, d)

    qw = N_HEADS * HEAD_DIM
    kw = N_KV_HEADS * HEAD_DIM
    hy0 = qw + 2 * kw
    g0 = hy0 + (HYENA_ORDER + 1) * d_hy
    def deinterleave(a):
        lead = a.shape[:-1]
        a = a.reshape(lead + (-1, HEAD_DIM // 2, 2))
        return jnp.swapaxes(a, -1, -2).reshape(lead + (-1,))

    w_f32, b_f32 = w_in[0], b_in[0].reshape(1, -1)
    w_main = jnp.concatenate([deinterleave(w_f32[:, :qw + kw]), w_f32[:, qw + kw:g0]], axis=1).astype(BF16)
    b_main = jnp.concatenate([deinterleave(b_f32[:, :qw + kw]), b_f32[:, qw + kw:g0]], axis=1)
    w_kv = w_main[:, qw:hy0]
    b_kv = b_main[:, qw:hy0]
    w_gates = w_f32[:, g0:].astype(BF16)
    b_gates = b_f32[:, g0:]
    qn = deinterleave(q_norm_w[0].reshape(1, HEAD_DIM)) * (math.log2(math.e) * HEAD_DIM ** -0.5)
    kn = deinterleave(k_norm_w[0].reshape(1, HEAD_DIM))
    cos_t, sin_t = _rope_tables(seq)
    n1w = norm1_w[0].reshape(1, d)
    n2w = norm2_w[0].reshape(1, d)

    q, k_all, v_all, u = _inproj(x, mod, n1w, w_main, b_main, qn, kn, cos_t, sin_t,
                                 hy_conv_w[0], hy_conv_b[0].reshape(1, -1), ctx_len)
    k_all, v_all = _ctx_kv(ctx, mod, n1w, w_kv, b_kv, kn, k_all, v_all, seq)
    attn_o = _attention(q, k_all, v_all)

    z2, tab, absdelta = _filter_tables(seq, d_hy)
    hid = filt_w1.shape[2]
    assert 2 * hid == LANES

    def lane_block(w, s):
        z = jnp.zeros_like(w)
        return jnp.concatenate([w, z] if s == 0 else [z, w], axis=1)

    def row_block(w, s):
        z = jnp.zeros_like(w)
        return jnp.concatenate([w, z] if s == 0 else [z, w], axis=0)

    w1p = jnp.zeros((LANES, hid), F32).at[:FILTER_EMB_DIM].set(filt_w1[0])
    w1_2 = jnp.stack([lane_block(w1p, 0), lane_block(w1p, 1)])
    wi_2 = jnp.stack([jnp.concatenate([lane_block(w, 0), lane_block(w, 1)], axis=0)
                      for w in filt_w_inner[0]])
    twice = lambda v: jnp.concatenate([v, v], axis=-1)
    w_out = filt_w_out[0].reshape(-1, 2, HYENA_ORDER * d_hy).astype(BF16)
    wf_2 = jnp.stack([row_block(w_out[:, 0], 0), row_block(w_out[:, 0], 1)])
    wb_2 = jnp.stack([row_block(w_out[:, 1], 0), row_block(w_out[:, 1], 1)])
    g_time = _filter_time(z2, tab, absdelta, w1_2, twice(filt_b1[0].reshape(1, -1)), wi_2,
                          twice(filt_b_inner[0][:, None, :]), twice(filt_freq[0].reshape(1, -1)),
                          wf_2, wb_2)
    ff_full, ff_half, fa, fai, gi, n_p = _dft_tables(seq)
    kspec = _filter_spectrum(g_time, ff_full, fa, n_p)
    skip = hy_skip[0].reshape(HYENA_ORDER, 1, d_hy)
    lanes_per_stream = d_hy // LANES
    z1 = _hyena_conv(u, 2 * lanes_per_stream, u, 0, kspec, 0, skip, ff_half, fa, fai, gi, d_hy)
    hy_o = _hyena_conv(z1, 0, u, lanes_per_stream, kspec, 1, skip, ff_half, fa, fai, gi, d_hy)

    x_new, h2 = _merge(x, mod, n1w, n2w, w_gates, b_gates, attn_o, hy_o,
                       w_attn_out[0].astype(BF16), w_hy_out[0].astype(BF16), w_o[0].astype(BF16),
                       b_o[0].reshape(1, d))
    return _ffn(h2, x_new, mod, w_up[0].astype(BF16), b_up[0].reshape(1, -1), ffn_conv_w[0],
                ffn_conv_b[0].reshape(1, -1), w_down[0].astype(BF16), b_down[0].reshape(1, d),
                final_norm_w.reshape(1, d))
```

```python
import functools
import math

import numpy as np
import jax
import jax.numpy as jnp
from jax import lax
from jax.experimental import pallas as pl
from jax.experimental.pallas import tpu as pltpu

F32 = jnp.float32
BF16 = jnp.bfloat16
HIGHEST = lax.Precision.HIGHEST

GRID_W = 64
N_HEADS = 8
N_KV_HEADS = 2
Q_GROUP = N_HEADS // N_KV_HEADS
HEAD_DIM = 128
ROPE_THETA = 10000.0
RMS_EPS = 1e-6
HYENA_ORDER = 2
FILTER_EMB_DIM = 33
DECAY_TARGET = 1e-2
FAST_DECAY_PCT = 0.3
SLOW_DECAY_PCT = 1.5
N_MOD = 6

LANES = 128
SUBLANES = 8
BF16_ROWS = 16
VMEM_LIMIT = 56 * 1024 * 1024

SLAB = 128
T_PITCH = SLAB + SUBLANES
S_PITCH = 2 * SLAB + SUBLANES

T_TOK = 512
T_FFN = 512
T_Q = 256
KV_SUB = 256
V_ROWS = HEAD_DIM + BF16_ROWS
ATTN_STEPS = 8
ATTN_QSUBS = 2
ATTN_PASSES = 2
CROSS_UNROLL = 16
SLAB_UNROLL = 8
T_FILT = 1024
FF_CHUNK = 256
HALO_F32 = SUBLANES
HALO_BF16 = BF16_ROWS


def _round_up(n, m):
    return (n + m - 1) // m * m


def _rms(x):
    return x * lax.rsqrt(jnp.mean(x * x, axis=-1, keepdims=True) + RMS_EPS)


def _mod_norm(x, w, shift, scale):
    return _rms(x) * (w * (1.0 + scale)) + shift


def _const_spec(shape):
    n = len(shape)
    return pl.BlockSpec(shape, lambda *_: (0,) * n, pipeline_mode=pl.Buffered(1))


def _modulation_kernel(c_ref, w_ref, b_ref, o_ref):
    c = c_ref[...]
    s = c * jax.nn.sigmoid(c)
    o_ref[...] = jnp.dot(s, w_ref[...], precision=HIGHEST, preferred_element_type=F32) + b_ref[...]


def _modulation(cvec, w_mod, b_mod):
    rows, d = cvec.shape
    width = w_mod.shape[1]
    blk = d
    return pl.pallas_call(
        _modulation_kernel,
        out_shape=jax.ShapeDtypeStruct((rows, width), F32),
        grid=(width // blk,),
        in_specs=[pl.BlockSpec((rows, d), lambda j: (0, 0)),
                  pl.BlockSpec((d, blk), lambda j: (0, j)),
                  pl.BlockSpec((1, blk), lambda j: (0, j))],
        out_specs=pl.BlockSpec((rows, blk), lambda j: (0, j)),
        compiler_params=pltpu.CompilerParams(dimension_semantics=("arbitrary",)),
        name="modulation",
    )(cvec, w_mod, b_mod.reshape(1, width))


def _head_norm(p, w):
    return _rms(p) * w


def _values_block(v):
    ones = jnp.ones((V_ROWS - HEAD_DIM, v.shape[0]), F32)
    return jnp.concatenate([v.T, ones], axis=0).astype(BF16)


def _rope(p, cos, sin):
    return p * cos + pltpu.roll(p, HEAD_DIM // 2, axis=1) * sin


def _inproj_kernel(xp_ref, xm_ref, xn_ref, mod_ref, n1_ref, w_ref, b_ref, qn_ref, kn_ref,
                   cos_ref, sin_ref, cw_ref, cb_ref, kin_ref, vin_ref,
                   q_ref, k_ref, v_ref, u_ref, pbuf, *, tok, d_hy3):
    del kin_ref, vin_ref
    i = pl.program_id(1)
    nt = pl.num_programs(1)
    shift = mod_ref[0, 0:1, :]
    scale = mod_ref[0, 1:2, :]
    n1 = n1_ref[...]
    hm = _mod_norm(xm_ref[0], n1, shift, scale).astype(BF16)
    halo = jnp.concatenate([xp_ref[0], xn_ref[0]], axis=0)
    hh = _mod_norm(halo, n1, shift, scale).astype(BF16)

    qw = N_HEADS * HEAD_DIM
    kw = N_KV_HEADS * HEAD_DIM
    hy0 = qw + 2 * kw
    cos = cos_ref[...]
    sin = sin_ref[...]
    qn = qn_ref[...]
    kn = kn_ref[...]

    p = jnp.dot(hm, w_ref[...], preferred_element_type=F32) + b_ref[...]
    for h in range(N_HEADS):
        ph = _rope(_head_norm(p[:, h * HEAD_DIM:(h + 1) * HEAD_DIM], qn), cos, sin)
        for j in range(tok // T_Q):
            q_ref[0, h, j] = ph[j * T_Q:(j + 1) * T_Q].T.astype(BF16)
    for h in range(N_KV_HEADS):
        ph = p[:, qw + h * HEAD_DIM: qw + (h + 1) * HEAD_DIM]
        k_ref[0, h] = _rope(_head_norm(ph, kn), cos, sin).astype(BF16)
        pv = p[:, qw + kw + h * HEAD_DIM: qw + kw + (h + 1) * HEAD_DIM]
        for j in range(tok // KV_SUB):
            v_ref[0, h, j] = _values_block(pv[j * KV_SUB:(j + 1) * KV_SUB])

    ph = jnp.dot(hh, w_ref[:, hy0:], preferred_element_type=F32) + b_ref[:, hy0:]
    top = jnp.where(i == 0, 0.0, ph[:HALO_F32])
    bot = jnp.where(i == nt - 1, 0.0, ph[HALO_F32:])
    pbuf[0:HALO_F32, :] = top
    pbuf[HALO_F32:HALO_F32 + tok, :] = p[:, hy0:]
    pbuf[HALO_F32 + tok:, :] = bot
    u = (cw_ref[0:1, :] * pbuf[HALO_F32 - 1:HALO_F32 - 1 + tok, :]
         + cw_ref[1:2, :] * pbuf[HALO_F32:HALO_F32 + tok, :]
         + cw_ref[2:3, :] * pbuf[HALO_F32 + 1:HALO_F32 + 1 + tok, :]
         + cb_ref[...])
    pad = jnp.zeros((T_PITCH - SLAB, d_hy3), F32)
    for j in range(tok // SLAB):
        u_ref[0, j * T_PITCH:j * T_PITCH + SLAB, :] = u[j * SLAB:(j + 1) * SLAB]
        u_ref[0, j * T_PITCH + SLAB:(j + 1) * T_PITCH, :] = pad


def _inproj(x, mod, n1w, w, b, qn, kn, cos_t, sin_t, cw, cb, ctx_len):
    bsz, seq, d = x.shape
    tok = T_TOK
    nt = seq // tok
    width = w.shape[1]
    qw = N_HEADS * HEAD_DIM
    kw = N_KV_HEADS * HEAD_DIM
    d_hy3 = width - qw - 2 * kw
    hb = tok // HALO_F32
    n_hblk = seq // HALO_F32
    kernel = functools.partial(_inproj_kernel, tok=tok, d_hy3=d_hy3)
    rows_p = (seq // SLAB) * T_PITCH
    out_shape = (
        jax.ShapeDtypeStruct((bsz, N_HEADS, seq // T_Q, HEAD_DIM, T_Q), BF16),
        jax.ShapeDtypeStruct((bsz, N_KV_HEADS, seq + ctx_len, HEAD_DIM), BF16),
        jax.ShapeDtypeStruct((bsz, N_KV_HEADS, (seq + ctx_len) // KV_SUB, V_ROWS, KV_SUB), BF16),
        jax.ShapeDtypeStruct((bsz, rows_p, d_hy3), F32),
    )
    return pl.pallas_call(
        kernel,
        out_shape=out_shape,
        grid=(bsz, nt),
        in_specs=[
            pl.BlockSpec((1, HALO_F32, d), lambda b_, i: (b_, jnp.maximum(i * hb - 1, 0), 0)),
            pl.BlockSpec((1, tok, d), lambda b_, i: (b_, i, 0)),
            pl.BlockSpec((1, HALO_F32, d), lambda b_, i: (b_, jnp.minimum((i + 1) * hb, n_hblk - 1), 0)),
            pl.BlockSpec((1, N_MOD, d), lambda b_, i: (b_, 0, 0)),
            _const_spec((1, d)),
            _const_spec((d, width)),
            _const_spec((1, width)),
            _const_spec((1, HEAD_DIM)),
            _const_spec((1, HEAD_DIM)),
            pl.BlockSpec((tok, HEAD_DIM), lambda b_, i: (i, 0)),
            pl.BlockSpec((tok, HEAD_DIM), lambda b_, i: (i, 0)),
            _const_spec((3, d_hy3)),
            _const_spec((1, d_hy3)),
            pl.BlockSpec(memory_space=pl.ANY), pl.BlockSpec(memory_space=pl.ANY),
        ],
        out_specs=(
            pl.BlockSpec((1, N_HEADS, tok // T_Q, HEAD_DIM, T_Q), lambda b_, i: (b_, 0, i, 0, 0)),
            pl.BlockSpec((1, N_KV_HEADS, tok, HEAD_DIM), lambda b_, i: (b_, 0, i, 0)),
            pl.BlockSpec((1, N_KV_HEADS, tok // KV_SUB, V_ROWS, KV_SUB), lambda b_, i: (b_, 0, i, 0, 0)),
            pl.BlockSpec((1, (tok // SLAB) * T_PITCH, d_hy3), lambda b_, i: (b_, i, 0)),
        ),
        scratch_shapes=[pltpu.VMEM((tok + 2 * HALO_F32, d_hy3), F32)],
        input_output_aliases={13: 1, 14: 2},
        compiler_params=pltpu.CompilerParams(
            dimension_semantics=("arbitrary", "arbitrary"), vmem_limit_bytes=VMEM_LIMIT),
        name="inproj",
    )(x, x, x, mod, n1w, w, b, qn, kn, cos_t, sin_t, cw, cb,
      jnp.zeros(out_shape[1].shape, BF16), jnp.zeros(out_shape[2].shape, BF16))


def _ctx_kv_kernel(c_ref, mod_ref, n1_ref, w_ref, b_ref, kn_ref, kin_ref, vin_ref, k_ref, v_ref):
    del kin_ref, vin_ref
    h = _mod_norm(c_ref[0], n1_ref[...], mod_ref[0, 0:1, :], mod_ref[0, 1:2, :]).astype(BF16)
    p = jnp.dot(h, w_ref[...], preferred_element_type=F32) + b_ref[...]
    kw = N_KV_HEADS * HEAD_DIM
    kn = kn_ref[...]
    for hd in range(N_KV_HEADS):
        k_ref[0, hd] = _head_norm(p[:, hd * HEAD_DIM:(hd + 1) * HEAD_DIM], kn).astype(BF16)
        pv = p[:, kw + hd * HEAD_DIM: kw + (hd + 1) * HEAD_DIM]
        for j in range(pv.shape[0] // KV_SUB):
            v_ref[0, hd, j] = _values_block(pv[j * KV_SUB:(j + 1) * KV_SUB])


def _ctx_kv(ctx, mod, n1w, w, b, kn, k_all, v_all, seq):
    bsz, ctx_len, d = ctx.shape
    ctx_row = bsz
    blk = seq // ctx_len
    any_spec = pl.BlockSpec(memory_space=pl.ANY)
    return pl.pallas_call(
        _ctx_kv_kernel,
        out_shape=(jax.ShapeDtypeStruct(k_all.shape, k_all.dtype),
                   jax.ShapeDtypeStruct(v_all.shape, v_all.dtype)),
        grid=(bsz,),
        in_specs=[
            pl.BlockSpec((1, ctx_len, d), lambda b_: (b_, 0, 0)),
            pl.BlockSpec((1, N_MOD, d), lambda b_: (ctx_row, 0, 0)),
            _const_spec((1, d)),
            _const_spec(w.shape),
            _const_spec(b.shape),
            _const_spec((1, HEAD_DIM)),
            any_spec, any_spec,
        ],
        out_specs=(
            pl.BlockSpec((1, N_KV_HEADS, ctx_len, HEAD_DIM), lambda b_: (b_, 0, blk, 0)),
            pl.BlockSpec((1, N_KV_HEADS, ctx_len // KV_SUB, V_ROWS, KV_SUB),
                         lambda b_: (b_, 0, seq // ctx_len, 0, 0)),
        ),
        input_output_aliases={6: 0, 7: 1},
        compiler_params=pltpu.CompilerParams(dimension_semantics=("arbitrary",)),
        name="ctx_kv",
    )(ctx, mod, n1w, w, b, kn, k_all, v_all)


def _attn_kernel(q_ref, k_ref, v_ref, o_ref, s_a, s_b, p_a, p_b, acc_ref, *, n_sub):
    tq = q_ref.shape[4]
    chains = [(g, j) for g in range(Q_GROUP) for j in range(ATTN_QSUBS)]
    assert n_sub % 2 == 1 and n_sub >= 3
    assert (n_sub - 1) % ATTN_STEPS == 0 and ATTN_STEPS % 2 == 0

    def one_pass(ps, carry):
        _attn_pass(ps * ATTN_QSUBS, chains, tq, n_sub, q_ref, k_ref, v_ref, o_ref, s_a, s_b, p_a, p_b, acc_ref)
        return carry

    lax.fori_loop(0, q_ref.shape[2] // ATTN_QSUBS, one_pass, 0)


def _attn_pass(q0, chains, tq, n_sub, q_ref, k_ref, v_ref, o_ref, s_a, s_b, p_a, p_b, acc_ref):
    def scores(c, s_out):
        start = c * KV_SUB
        if not isinstance(c, int):
            start = pl.multiple_of(start, KV_SUB)
        kc = k_ref[0, 0, pl.ds(start, KV_SUB), :]
        for ch, (g, j) in enumerate(chains):
            s_out[ch] = jnp.dot(kc, q_ref[0, g, q0 + j], preferred_element_type=F32)

    def softmax(s_in, p_out, stats):
        new_stats, alphas = [], []
        for ch in range(len(chains)):
            s = s_in[ch]
            m_new = jnp.maximum(stats[ch], jnp.max(s, axis=0, keepdims=True))
            alphas.append(jnp.exp2(stats[ch] - m_new))
            p_out[ch] = jnp.exp2(s - m_new).astype(BF16)
            new_stats.append(m_new)
        return tuple(new_stats), tuple(alphas)

    def values(c, p_in, alphas):
        vt = v_ref[0, 0, c]
        for ch in range(len(chains)):
            acc_ref[ch] = alphas[ch] * acc_ref[ch] + jnp.dot(vt, p_in[ch], preferred_element_type=F32)

    acc_ref[...] = jnp.zeros(acc_ref.shape, F32)
    stats = (jnp.full((1, tq), -jnp.inf, F32),) * len(chains)
    scores(0, s_a)
    scores(1, s_b)
    stats, alphas = softmax(s_a, p_a, stats)

    def group(t, carry):
        stats, alphas = carry
        for u in range(ATTN_STEPS):
            c = ATTN_STEPS * t + 1 + u
            s_next, s_cur, p_cur, p_prev = (s_a, s_b, p_b, p_a) if u % 2 == 0 else (s_b, s_a, p_a, p_b)
            scores(jnp.minimum(c + 1, n_sub - 1), s_next)
            stats, alphas_new = softmax(s_cur, p_cur, stats)
            values(c - 1, p_prev, alphas)
            alphas = alphas_new
        return stats, alphas

    stats, alphas = lax.fori_loop(0, (n_sub - 1) // ATTN_STEPS, group, (stats, alphas))
    values(n_sub - 1, p_a, alphas)
    for ch, (g, j) in enumerate(chains):
        out = acc_ref[ch, :HEAD_DIM, :] / acc_ref[ch, HEAD_DIM:HEAD_DIM + 1, :]
        row = pl.multiple_of((q0 + j) * tq, tq)
        o_ref[0, pl.ds(row, tq), g * HEAD_DIM:(g + 1) * HEAD_DIM] = out.T.astype(o_ref.dtype)


def _attention(q, k_all, v_all):
    bsz, _, n_qsub, _, tq = q.shape
    n_keys = k_all.shape[2]
    n_sub = v_all.shape[2]
    qs = ATTN_QSUBS * ATTN_PASSES
    n_chain = Q_GROUP * ATTN_QSUBS
    assert n_qsub % qs == 0
    kernel = functools.partial(_attn_kernel, n_sub=n_sub)
    return pl.pallas_call(
        kernel,
        out_shape=jax.ShapeDtypeStruct((bsz, n_qsub * tq, N_HEADS * HEAD_DIM), BF16),
        grid=(bsz, N_KV_HEADS, n_qsub // qs),
        in_specs=[
            pl.BlockSpec((1, Q_GROUP, qs, HEAD_DIM, tq), lambda b_, h, i: (b_, h, i, 0, 0)),
            pl.BlockSpec((1, 1, n_keys, HEAD_DIM), lambda b_, h, i: (b_, h, 0, 0)),
            pl.BlockSpec((1, 1, n_sub, V_ROWS, KV_SUB), lambda b_, h, i: (b_, h, 0, 0, 0)),
        ],
        out_specs=pl.BlockSpec((1, qs * tq, Q_GROUP * HEAD_DIM), lambda b_, h, i: (b_, i, h)),
        scratch_shapes=[
            pltpu.VMEM((n_chain, KV_SUB, tq), F32), pltpu.VMEM((n_chain, KV_SUB, tq), F32),
            pltpu.VMEM((n_chain, KV_SUB, tq), BF16), pltpu.VMEM((n_chain, KV_SUB, tq), BF16),
            pltpu.VMEM((n_chain, V_ROWS, tq), F32),
        ],
        compiler_params=pltpu.CompilerParams(
            dimension_semantics=("arbitrary", "arbitrary", "arbitrary"), vmem_limit_bytes=VMEM_LIMIT),
        name="attention",
    )(q, k_all, v_all)


def _filter_kernel(z_ref, tab_ref, dl_ref, w1_ref, b1_ref, wi_ref, bi_ref, fr_ref, wf_ref, wb_ref,
                   g_ref, *, rows, d_hy, n_inner):
    half = rows // 2
    fr = fr_ref[...]
    pre = (jnp.dot(z_ref[:half, :], w1_ref[0], precision=HIGHEST, preferred_element_type=F32)
           + jnp.dot(z_ref[half:, :], w1_ref[1], precision=HIGHEST, preferred_element_type=F32))
    h = jnp.sin(fr * (pre + b1_ref[...]))
    for j in range(n_inner):
        h = jnp.sin(fr * (jnp.dot(h, wi_ref[j], precision=HIGHEST,
                                  preferred_element_type=F32) + bi_ref[j]))
    hb = h.astype(BF16)
    pad = jnp.zeros((T_PITCH - SLAB, d_hy), F32)
    for part in range(2):
        pf = jnp.dot(hb, wf_ref[part], preferred_element_type=F32)
        pb = jnp.dot(hb, wb_ref[part], preferred_element_type=F32)
        r0 = part * half
        t = tab_ref[r0:r0 + half, 0:1]
        mf = tab_ref[r0:r0 + half, 1:2]
        mb = tab_ref[r0:r0 + half, 2:3]
        decay = jnp.exp(-t * dl_ref[...])
        for o in range(HYENA_ORDER):
            g = (mf * pf[:, o * d_hy:(o + 1) * d_hy] + mb * pb[:, o * d_hy:(o + 1) * d_hy]) * decay
            for j in range(half // SLAB):
                row = (r0 // SLAB + j) * T_PITCH
                g_ref[o, row:row + SLAB, :] = g[j * SLAB:(j + 1) * SLAB]
                g_ref[o, row + SLAB:row + T_PITCH, :] = pad


def _filter_time(z2, tab, absdelta, w1p, b1, wi, bi, freq, wf, wb):
    n_rows = z2.shape[0]
    d_hy = absdelta.shape[1]
    rows = T_FILT
    n_inner = wi.shape[0]
    kernel = functools.partial(_filter_kernel, rows=rows, d_hy=d_hy, n_inner=n_inner)
    return pl.pallas_call(
        kernel,
        out_shape=jax.ShapeDtypeStruct((HYENA_ORDER, (n_rows // SLAB) * T_PITCH, d_hy), F32),
        grid=(n_rows // rows,),
        in_specs=[
            pl.BlockSpec((rows, z2.shape[1]), lambda i: (i, 0)),
            pl.BlockSpec((rows, tab.shape[1]), lambda i: (i, 0)),
            _const_spec(absdelta.shape), _const_spec(w1p.shape), _const_spec(b1.shape),
            _const_spec(wi.shape), _const_spec(bi.shape), _const_spec(freq.shape),
            _const_spec(wf.shape), _const_spec(wb.shape),
        ],
        out_specs=pl.BlockSpec((HYENA_ORDER, (rows // SLAB) * T_PITCH, d_hy), lambda i: (0, i, 0)),
        compiler_params=pltpu.CompilerParams(
            dimension_semantics=("arbitrary",), vmem_limit_bytes=VMEM_LIMIT),
        name="hyena_filter",
    )(z2, tab, absdelta, w1p, b1, wi, bi, freq, wf, wb)


def _loop(n, body, unroll):
    def group(t, carry):
        for j in range(unroll):
            body(t * unroll + j)
        return carry

    lax.fori_loop(0, n // unroll, group, 0)
    for i in range(n - n % unroll, n):
        body(i)


def _forward_cross_slab(src_ref, n_in, ff_ref, s_ref, half):
    def body(r):
        g = src_ref[pl.ds(r, n_in, stride=T_PITCH), :]
        a = jnp.dot(ff_ref[r], g.astype(BF16), preferred_element_type=F32)
        s_ref[pl.ds(r, half, stride=S_PITCH), :] = a[:half]
        s_ref[pl.ds(SLAB + r, half, stride=S_PITCH), :] = a[half:]

    _loop(SLAB, body, CROSS_UNROLL)


def _filter_spec_kernel(g_ref, ff_ref, fa_ref, k_ref, s_ref, *, n_in, n_p, half):
    _forward_cross_slab(g_ref.at[0], n_in, ff_ref, s_ref, half)
    fa = fa_ref[...]

    def slab(p):
        start = p * S_PITCH
        if not isinstance(p, int):
            start = pl.multiple_of(start, SUBLANES)
        a = s_ref[pl.ds(start, 2 * SLAB), :]
        k_ref[0, p] = jnp.dot(fa, a.astype(BF16), preferred_element_type=F32).astype(k_ref.dtype)

    _loop(n_p, slab, SLAB_UNROLL)


def _filter_spectrum(g, ff_full, fa, n_p):
    n_ord, rows_p, d_hy = g.shape
    n_in = rows_p // T_PITCH
    half = ff_full.shape[1] // 2
    kernel = functools.partial(_filter_spec_kernel, n_in=n_in, n_p=n_p, half=half)
    return pl.pallas_call(
        kernel,
        out_shape=jax.ShapeDtypeStruct((n_ord, n_p, 2 * SLAB, d_hy), BF16),
        grid=(n_ord, d_hy // LANES),
        in_specs=[
            pl.BlockSpec((1, rows_p, LANES), lambda o, c: (o, 0, c)),
            _const_spec(ff_full.shape),
            _const_spec(fa.shape),
        ],
        out_specs=pl.BlockSpec((1, n_p, 2 * SLAB, LANES), lambda o, c: (o, 0, 0, c)),
        scratch_shapes=[pltpu.VMEM((half * S_PITCH, LANES), F32)],
        compiler_params=pltpu.CompilerParams(
            dimension_semantics=("arbitrary", "arbitrary"), vmem_limit_bytes=VMEM_LIMIT),
        name="hyena_filter_spectrum",
    )(g, ff_full, fa)


def _conv_kernel(z_ref, x_ref, k_ref, sk_ref, ff_ref, fa_ref, fai_ref, gi_ref, o_ref, s_ref,
                 *, n_a, n_p, half):
    _forward_cross_slab(z_ref.at[0], n_a, ff_ref, s_ref, half)
    fa = fa_ref[...]
    fai = fai_ref[...]

    def slab(p):
        start = p * S_PITCH
        if not isinstance(p, int):
            start = pl.multiple_of(start, SUBLANES)
        a = s_ref[pl.ds(start, 2 * SLAB), :]
        x = jnp.dot(fa, a.astype(BF16), preferred_element_type=F32)
        kk = k_ref[0, p].astype(F32)
        xr, xi = x[:SLAB], x[SLAB:]
        kr, ki = kk[:SLAB], kk[SLAB:]
        y = jnp.concatenate([xr * kr - xi * ki, xr * ki + xi * kr], axis=0)
        s_ref[pl.ds(start, 2 * SLAB), :] = jnp.dot(fai, y.astype(BF16), preferred_element_type=F32)

    _loop(n_p, slab, SLAB_UNROLL)
    skip = sk_ref[0]

    def inverse(r):
        br = s_ref[pl.ds(r, half, stride=S_PITCH), :]
        bi = s_ref[pl.ds(SLAB + r, half, stride=S_PITCH), :]
        bb = jnp.concatenate([br, bi], axis=0).astype(BF16)
        y = jnp.dot(gi_ref[r], bb, preferred_element_type=F32)
        z = z_ref[0, pl.ds(r, n_a, stride=T_PITCH), :]
        gate = x_ref[0, pl.ds(r, n_a, stride=T_PITCH), :]
        o_ref[0, pl.ds(r, n_a, stride=T_PITCH), :] = gate * (y + skip * z)

    _loop(SLAB, inverse, CROSS_UNROLL)
    pad = jnp.zeros((T_PITCH - SLAB, LANES), F32)
    for a in range(n_a):
        o_ref[0, a * T_PITCH + SLAB:(a + 1) * T_PITCH, :] = pad


def _hyena_conv(zin, z_blk0, gate, g_blk0, kspec, order, skip, ff, fa, fai, gi, d_hy):
    bsz, rows_p, _ = zin.shape
    n_a = rows_p // T_PITCH
    n_p = kspec.shape[1]
    half = ff.shape[1] // 2
    kernel = functools.partial(_conv_kernel, n_a=n_a, n_p=n_p, half=half)
    return pl.pallas_call(
        kernel,
        out_shape=jax.ShapeDtypeStruct((bsz, rows_p, d_hy), F32),
        grid=(d_hy // LANES, bsz),
        in_specs=[
            pl.BlockSpec((1, rows_p, LANES), lambda c, b_: (b_, 0, z_blk0 + c)),
            pl.BlockSpec((1, rows_p, LANES), lambda c, b_: (b_, 0, g_blk0 + c)),
            pl.BlockSpec((1, n_p, 2 * SLAB, LANES), lambda c, b_: (order, 0, 0, c),
                         pipeline_mode=pl.Buffered(1)),
            pl.BlockSpec((1, 1, LANES), lambda c, b_: (order, 0, c)),
            _const_spec(ff.shape), _const_spec(fa.shape), _const_spec(fai.shape),
            _const_spec(gi.shape),
        ],
        out_specs=pl.BlockSpec((1, rows_p, LANES), lambda c, b_: (b_, 0, c)),
        scratch_shapes=[pltpu.VMEM((half * S_PITCH, LANES), F32)],
        compiler_params=pltpu.CompilerParams(
            dimension_semantics=("arbitrary", "arbitrary"), vmem_limit_bytes=VMEM_LIMIT),
        name=f"hyena_conv{order}",
    )(zin, gate, kspec, skip, ff, fa, fai, gi)


def _merge_kernel(x_ref, mod_ref, n1_ref, n2_ref, wg_ref, bg_ref, a_ref, hy_ref, wa_ref, wh_ref,
                  wo_ref, bo_ref, xo_ref, h2_ref, *, tok, d):
    x = x_ref[0]
    sh1, sc1, g1 = mod_ref[0, 0:1, :], mod_ref[0, 1:2, :], mod_ref[0, 2:3, :]
    sh2, sc2 = mod_ref[0, 3:4, :], mod_ref[0, 4:5, :]
    h1 = _mod_norm(x, n1_ref[...], sh1, sc1).astype(BF16)
    gates = jnp.dot(h1, wg_ref[...], preferred_element_type=F32) + bg_ref[...]
    hy = jnp.concatenate([hy_ref[0, j * T_PITCH:j * T_PITCH + SLAB, :] for j in range(tok // SLAB)],
                         axis=0)
    pa = jnp.dot(a_ref[0], wa_ref[...], preferred_element_type=F32)
    ph = jnp.dot(hy.astype(BF16), wh_ref[...], preferred_element_type=F32)
    mixed = jax.nn.sigmoid(gates[:, :d]) * pa + jax.nn.sigmoid(gates[:, d:]) * ph
    y = jnp.dot(mixed.astype(BF16), wo_ref[...], preferred_element_type=F32) + bo_ref[...]
    xn = x + g1 * y
    xo_ref[0] = xn
    h2_ref[0] = _mod_norm(xn, n2_ref[...], sh2, sc2).astype(BF16)


def _merge(x, mod, n1w, n2w, wg, bg, attn_o, hy_o, wa, wh, wo, bo):
    bsz, seq, d = x.shape
    tok = T_TOK
    d_hy = hy_o.shape[2]
    kernel = functools.partial(_merge_kernel, tok=tok, d=d)
    return pl.pallas_call(
        kernel,
        out_shape=(jax.ShapeDtypeStruct((bsz, seq, d), F32), jax.ShapeDtypeStruct((bsz, seq, d), BF16)),
        grid=(bsz, seq // tok),
        in_specs=[
            pl.BlockSpec((1, tok, d), lambda b_, i: (b_, i, 0)),
            pl.BlockSpec((1, N_MOD, d), lambda b_, i: (b_, 0, 0)),
            _const_spec((1, d)), _const_spec((1, d)),
            _const_spec(wg.shape), _const_spec(bg.shape),
            pl.BlockSpec((1, tok, attn_o.shape[2]), lambda b_, i: (b_, i, 0)),
            pl.BlockSpec((1, (tok // SLAB) * T_PITCH, d_hy), lambda b_, i: (b_, i, 0)),
            _const_spec(wa.shape), _const_spec(wh.shape), _const_spec(wo.shape), _const_spec(bo.shape),
        ],
        out_specs=(pl.BlockSpec((1, tok, d), lambda b_, i: (b_, i, 0)),
                   pl.BlockSpec((1, tok, d), lambda b_, i: (b_, i, 0))),
        compiler_params=pltpu.CompilerParams(
            dimension_semantics=("arbitrary", "arbitrary"), vmem_limit_bytes=VMEM_LIMIT),
        name="merge",
    )(x, mod, n1w, n2w, wg, bg, attn_o, hy_o, wa, wh, wo, bo)


def _gelu_tanh(x):
    return 0.5 * x * (1.0 + jnp.tanh(math.sqrt(2.0 / math.pi) * (x + 0.044715 * (x * x * x))))


def _ffn_kernel(hp_ref, hm_ref, hn_ref, x_ref, mod_ref, wu_ref, bu_ref, cw_ref, cb_ref, wd_ref, bd_ref,
                fn_ref, o_ref, pa_buf, pg_buf, act_buf, *, tok, d_ff):
    i = pl.program_id(1)
    nt = pl.num_programs(1)
    hext = jnp.concatenate([hp_ref[0], hm_ref[0], hn_ref[0]], axis=0)
    lo = HALO_BF16

    def fill(buf, col0):
        cs = slice(col0, col0 + FF_CHUNK)
        p = jnp.dot(hext, wu_ref[:, cs], preferred_element_type=F32)
        nb = -bu_ref[:, cs]
        buf[0:lo, :] = jnp.where(i == 0, nb, p[0:lo])
        buf[lo:lo + tok, :] = p[lo:lo + tok]
        buf[lo + tok:, :] = jnp.where(i == nt - 1, nb, p[lo + tok:])

    def conv(buf, col0):
        cs = slice(col0, col0 + FF_CHUNK)
        w0, w1, w2 = cw_ref[0:1, cs], cw_ref[1:2, cs], cw_ref[2:3, cs]
        const = (w0 + w1 + w2) * bu_ref[:, cs] + cb_ref[:, cs]
        return (w0 * buf[lo - 1:lo - 1 + tok, :] + w1 * buf[lo:lo + tok, :]
                + w2 * buf[lo + 1:lo + 1 + tok, :] + const)

    for c in range(d_ff // FF_CHUNK):
        ca = c * FF_CHUNK
        cg = d_ff + c * FF_CHUNK
        fill(pa_buf, ca)
        fill(pg_buf, cg)
        act_buf[:, ca:ca + FF_CHUNK] = (_gelu_tanh(conv(pa_buf, ca)) * conv(pg_buf, cg)).astype(BF16)
    y = jnp.dot(act_buf[...], wd_ref[...], preferred_element_type=F32)
    g2 = mod_ref[0, 5:6, :]
    xn = x_ref[0] + g2 * (y + bd_ref[...])
    o_ref[0] = _rms(xn) * fn_ref[...]


def _ffn(h2, x_new, mod, wu, bu, cw, cb, wd, bd, fnw):
    bsz, seq, d = x_new.shape
    tok = T_FFN
    d_ff = wd.shape[0]
    hb = tok // HALO_BF16
    n_hblk = seq // HALO_BF16
    rows = tok + 2 * HALO_BF16
    kernel = functools.partial(_ffn_kernel, tok=tok, d_ff=d_ff)
    return pl.pallas_call(
        kernel,
        out_shape=jax.ShapeDtypeStruct((bsz, seq, d), F32),
        grid=(bsz, seq // tok),
        in_specs=[
            pl.BlockSpec((1, HALO_BF16, d), lambda b_, i: (b_, jnp.maximum(i * hb - 1, 0), 0)),
            pl.BlockSpec((1, tok, d), lambda b_, i: (b_, i, 0)),
            pl.BlockSpec((1, HALO_BF16, d), lambda b_, i: (b_, jnp.minimum((i + 1) * hb, n_hblk - 1), 0)),
            pl.BlockSpec((1, tok, d), lambda b_, i: (b_, i, 0)),
            pl.BlockSpec((1, N_MOD, d), lambda b_, i: (b_, 0, 0)),
            _const_spec(wu.shape), _const_spec(bu.shape), _const_spec(cw.shape), _const_spec(cb.shape),
            _const_spec(wd.shape), _const_spec(bd.shape), _const_spec(fnw.shape),
        ],
        out_specs=pl.BlockSpec((1, tok, d), lambda b_, i: (b_, i, 0)),
        scratch_shapes=[pltpu.VMEM((rows, FF_CHUNK), F32), pltpu.VMEM((rows, FF_CHUNK), F32),
                        pltpu.VMEM((tok, d_ff), BF16)],
        compiler_params=pltpu.CompilerParams(
            dimension_semantics=("arbitrary", "arbitrary"), vmem_limit_bytes=VMEM_LIMIT),
        name="conv_ffn",
    )(h2, h2, h2, x_new, mod, wu, bu, cw, cb, wd, bd, fnw)


def _rope_tables(seq):
    half = HEAD_DIM // 2
    t = np.arange(seq)
    row = (t // GRID_W).astype(np.float64)
    col = (t % GRID_W).astype(np.float64)
    freqs = ROPE_THETA ** (-np.arange(0, half, 2, dtype=np.float64) / half)
    ang = np.concatenate([row[:, None] * freqs, col[:, None] * freqs], axis=-1)
    cos = np.cos(ang)
    sin = np.sin(ang)
    cos_t = np.concatenate([cos, cos], axis=-1)
    sin_t = np.concatenate([-sin, sin], axis=-1)
    return jnp.asarray(cos_t, F32), jnp.asarray(sin_t, F32)


def _filter_tables(seq, d_hy):
    n = 2 * seq
    j = np.arange(n)
    lag = np.where(j < seq, j, n - j)
    t = lag / max(seq - 1, 1)
    bands = (FILTER_EMB_DIM - 1) // 2
    f = np.linspace(1e-4, bands - 1, bands)
    wpos = 2.0 * math.pi * lag / seq
    z = np.concatenate([t[:, None], np.cos(wpos[:, None] * f), -np.sin(wpos[:, None] * f)], axis=-1)
    z2 = np.zeros((n, LANES), np.float64)
    z2[:, :FILTER_EMB_DIM] = z
    tab = np.zeros((n, SUBLANES), np.float64)
    tab[:, 0] = t
    tab[:, 1] = (j < seq)
    tab[:, 2] = np.logical_or(j == 0, j > seq)
    min_decay = math.log(DECAY_TARGET) / SLOW_DECAY_PCT
    max_decay = math.log(DECAY_TARGET) / FAST_DECAY_PCT
    absdelta = np.abs(np.linspace(min_decay, max_decay, d_hy))[None, :]
    return jnp.asarray(z2, F32), jnp.asarray(tab, F32), jnp.asarray(absdelta, F32)


def _dft_tables(seq):
    n_a = seq // SLAB
    n_s = 2 * n_a
    n = 2 * seq
    n_p = n_a + 1
    half = _round_up(n_p, SUBLANES)
    p = np.arange(half)
    live = (p < n_p).astype(np.float64)
    r = np.arange(SLAB)
    a = np.arange(n_s)
    theta = 2.0 * math.pi * p[None, :, None] * (SLAB * a[None, None, :] + r[:, None, None]) / n
    ff = np.concatenate([np.cos(theta) * live[None, :, None], -np.sin(theta) * live[None, :, None]], axis=1)
    phi = 2.0 * math.pi * np.outer(r, r) / SLAB
    c, s = np.cos(phi), np.sin(phi)
    fa = np.block([[c, s], [-s, c]])
    fai = np.block([[c, -s], [s, c]])
    wgt = np.where((p == 0) | (p == n_a), 1.0, 2.0) * live / n
    th_i = 2.0 * math.pi * (SLAB * a[None, :n_a, None] + r[:, None, None]) * p[None, None, :] / n
    gi = np.concatenate([np.cos(th_i) * wgt[None, None, :], -np.sin(th_i) * wgt[None, None, :]], axis=2)
    to_bf = lambda m: jnp.asarray(m, F32).astype(BF16)
    return to_bf(ff), to_bf(ff[:, :, :n_a]), to_bf(fa), to_bf(fai), to_bf(gi), n_p


def kernel(x, c, ctx, c_ctx, w_mod, b_mod, norm1_w, norm2_w, w_in, b_in, q_norm_w, k_norm_w, hy_conv_w, hy_conv_b, filt_w1, filt_b1, filt_w_inner, filt_b_inner, filt_freq, filt_w_out, hy_skip, w_attn_out, w_hy_out, w_o, b_o, w_up, b_up, ffn_conv_w, ffn_conv_b, w_down, b_down, final_norm_w):
    bsz, seq, d = x.shape
    ctx_len = ctx.shape[1]
    d_hy = hy_skip.shape[2]
    assert w_mod.shape[0] == 1, "single-layer kernel"
    assert seq % T_TOK == 0 and seq % T_Q == 0 and seq % ctx_len == 0 and (2 * seq) % T_FILT == 0
    assert T_TOK % KV_SUB == 0 and ctx_len % KV_SUB == 0 and seq % T_FFN == 0
    assert d_hy % LANES == 0 and w_down.shape[1] % FF_CHUNK == 0 and bsz + 1 <= SUBLANES

    cvec = jnp.zeros((SUBLANES, d), F32).at[:bsz].set(c).at[bsz].set(c_ctx)
    mod = _modulation(cvec, w_mod[0], b_mod[0]).reshape(SUBLANES, N_MOD, d)

    qw = N_HEADS * HEAD_DIM
    kw = N_KV_HEADS * HEAD_DIM
    hy0 = qw + 2 * kw
    g0 = hy0 + (HYENA_ORDER + 1) * d_hy
    def deinterleave(a):
        lead = a.shape[:-1]
        a = a.reshape(lead + (-1, HEAD_DIM // 2, 2))
        return jnp.swapaxes(a, -1, -2).reshape(lead + (-1,))

    w_f32, b_f32 = w_in[0], b_in[0].reshape(1, -1)
    w_main = jnp.concatenate([deinterleave(w_f32[:, :qw + kw]), w_f32[:, qw + kw:g0]], axis=1).astype(BF16)
    b_main = jnp.concatenate([deinterleave(b_f32[:, :qw + kw]), b_f32[:, qw + kw:g0]], axis=1)
    w_kv = w_main[:, qw:hy0]
    b_kv = b_main[:, qw:hy0]
    w_gates = w_f32[:, g0:].astype(BF16)
    b_gates = b_f32[:, g0:]
    qn = deinterleave(q_norm_w[0].reshape(1, HEAD_DIM)) * (math.log2(math.e) * HEAD_DIM ** -0.5)
    kn = deinterleave(k_norm_w[0].reshape(1, HEAD_DIM))
    cos_t, sin_t = _rope_tables(seq)
    n1w = norm1_w[0].reshape(1, d)
    n2w = norm2_w[0].reshape(1, d)

    q, k_all, v_all, u = _inproj(x, mod, n1w, w_main, b_main, qn, kn, cos_t, sin_t,
                                 hy_conv_w[0], hy_conv_b[0].reshape(1, -1), ctx_len)
    k_all, v_all = _ctx_kv(ctx, mod, n1w, w_kv, b_kv, kn, k_all, v_all, seq)
    attn_o = _attention(q, k_all, v_all)

    z2, tab, absdelta = _filter_tables(seq, d_hy)
    hid = filt_w1.shape[2]
    assert 2 * hid == LANES

    def lane_block(w, s):
        z = jnp.zeros_like(w)
        return jnp.concatenate([w, z] if s == 0 else [z, w], axis=1)

    def row_block(w, s):
        z = jnp.zeros_like(w)
        return jnp.concatenate([w, z] if s == 0 else [z, w], axis=0)

    w1p = jnp.zeros((LANES, hid), F32).at[:FILTER_EMB_DIM].set(filt_w1[0])
    w1_2 = jnp.stack([lane_block(w1p, 0), lane_block(w1p, 1)])
    wi_2 = jnp.stack([jnp.concatenate([lane_block(w, 0), lane_block(w, 1)], axis=0)
                      for w in filt_w_inner[0]])
    twice = lambda v: jnp.concatenate([v, v], axis=-1)
    w_out = filt_w_out[0].reshape(-1, 2, HYENA_ORDER * d_hy).astype(BF16)
    wf_2 = jnp.stack([row_block(w_out[:, 0], 0), row_block(w_out[:, 0], 1)])
    wb_2 = jnp.stack([row_block(w_out[:, 1], 0), row_block(w_out[:, 1], 1)])
    g_time = _filter_time(z2, tab, absdelta, w1_2, twice(filt_b1[0].reshape(1, -1)), wi_2,
                          twice(filt_b_inner[0][:, None, :]), twice(filt_freq[0].reshape(1, -1)),
                          wf_2, wb_2)
    ff_full, ff_half, fa, fai, gi, n_p = _dft_tables(seq)
    kspec = _filter_spectrum(g_time, ff_full, fa, n_p)
    skip = hy_skip[0].reshape(HYENA_ORDER, 1, d_hy)
    lanes_per_stream = d_hy // LANES
    z1 = _hyena_conv(u, 2 * lanes_per_stream, u, 0, kspec, 0, skip, ff_half, fa, fai, gi, d_hy)
    hy_o = _hyena_conv(z1, 0, u, lanes_per_stream, kspec, 1, skip, ff_half, fa, fai, gi, d_hy)

    x_new, h2 = _merge(x, mod, n1w, n2w, w_gates, b_gates, attn_o, hy_o,
                       w_attn_out[0].astype(BF16), w_hy_out[0].astype(BF16), w_o[0].astype(BF16),
                       b_o[0].reshape(1, d))
    return _ffn(h2, x_new, mod, w_up[0].astype(BF16), b_up[0].reshape(1, -1), ffn_conv_w[0],
                ffn_conv_b[0].reshape(1, -1), w_down[0].astype(BF16), b_down[0].reshape(1, d),
                final_norm_w.reshape(1, d))
```

```python
import functools
import math

import numpy as np
import jax
import jax.numpy as jnp
from jax import lax
from jax.experimental import pallas as pl
from jax.experimental.pallas import tpu as pltpu

F32 = jnp.float32
BF16 = jnp.bfloat16
HIGHEST = lax.Precision.HIGHEST

GRID_W = 64
N_HEADS = 8
N_KV_HEADS = 2
Q_GROUP = N_HEADS // N_KV_HEADS
HEAD_DIM = 128
ROPE_THETA = 10000.0
RMS_EPS = 1e-6
HYENA_ORDER = 2
FILTER_EMB_DIM = 33
DECAY_TARGET = 1e-2
FAST_DECAY_PCT = 0.3
SLOW_DECAY_PCT = 1.5
N_MOD = 6

LANES = 128
SUBLANES = 8
BF16_ROWS = 16
VMEM_LIMIT = 56 * 1024 * 1024

SLAB = 128
T_PITCH = SLAB + SUBLANES
S_PITCH = 2 * SLAB + SUBLANES

T_TOK = 512
T_FFN = 512
T_Q = 256
KV_SUB = 256
V_ROWS = HEAD_DIM + BF16_ROWS
ATTN_STEPS = 8
ATTN_QSUBS = 2
ATTN_PASSES = 2
CROSS_UNROLL = 16
SLAB_UNROLL = 8
SLAB_GROUP = 2
T_FILT = 1024
FF_CHUNK = 256
HALO_F32 = SUBLANES
HALO_BF16 = BF16_ROWS


def _round_up(n, m):
    return (n + m - 1) // m * m


def _rms(x):
    return x * lax.rsqrt(jnp.mean(x * x, axis=-1, keepdims=True) + RMS_EPS)


def _mod_norm(x, w, shift, scale):
    return _rms(x) * (w * (1.0 + scale)) + shift


def _dwconv3(ext, lo, n, w0, w1, w2, const):
    rows = ext.shape[0]
    prev = pltpu.roll(ext, 1, axis=0)[lo:lo + n]
    nxt = pltpu.roll(ext, rows - 1, axis=0)[lo:lo + n]
    return w0 * prev + w1 * ext[lo:lo + n] + w2 * nxt + const


def _const_spec(shape):
    n = len(shape)
    return pl.BlockSpec(shape, lambda *_: (0,) * n, pipeline_mode=pl.Buffered(1))


def _modulation_kernel(c_ref, w_ref, b_ref, o_ref):
    c = c_ref[...]
    s = c * jax.nn.sigmoid(c)
    o_ref[...] = jnp.dot(s, w_ref[...], precision=HIGHEST, preferred_element_type=F32) + b_ref[...]


def _modulation(cvec, w_mod, b_mod):
    rows, d = cvec.shape
    width = w_mod.shape[1]
    blk = d
    return pl.pallas_call(
        _modulation_kernel,
        out_shape=jax.ShapeDtypeStruct((rows, width), F32),
        grid=(width // blk,),
        in_specs=[pl.BlockSpec((rows, d), lambda j: (0, 0)),
                  pl.BlockSpec((d, blk), lambda j: (0, j)),
                  pl.BlockSpec((1, blk), lambda j: (0, j))],
        out_specs=pl.BlockSpec((rows, blk), lambda j: (0, j)),
        compiler_params=pltpu.CompilerParams(dimension_semantics=("arbitrary",)),
        name="modulation",
    )(cvec, w_mod, b_mod.reshape(1, width))


def _head_norm(p, w):
    return _rms(p) * w


def _values_block(v):
    ones = jnp.ones((V_ROWS - HEAD_DIM, v.shape[0]), F32)
    return jnp.concatenate([v.T, ones], axis=0).astype(BF16)


def _rope(p, cos, sin):
    return p * cos + pltpu.roll(p, HEAD_DIM // 2, axis=1) * sin


def _inproj_kernel(xp_ref, xm_ref, xn_ref, mod_ref, n1_ref, w_ref, b_ref, qn_ref, kn_ref,
                   cos_ref, sin_ref, cw_ref, cb_ref, kin_ref, vin_ref,
                   q_ref, k_ref, v_ref, u_ref, pbuf, *, tok, d_hy3):
    del kin_ref, vin_ref
    i = pl.program_id(1)
    nt = pl.num_programs(1)
    shift = mod_ref[0, 0:1, :]
    scale = mod_ref[0, 1:2, :]
    n1 = n1_ref[...]
    hm = _mod_norm(xm_ref[0], n1, shift, scale).astype(BF16)
    halo = jnp.concatenate([xp_ref[0], xn_ref[0]], axis=0)
    hh = _mod_norm(halo, n1, shift, scale).astype(BF16)

    qw = N_HEADS * HEAD_DIM
    kw = N_KV_HEADS * HEAD_DIM
    hy0 = qw + 2 * kw
    cos = cos_ref[...]
    sin = sin_ref[...]
    qn = qn_ref[...]
    kn = kn_ref[...]

    p = jnp.dot(hm, w_ref[...], preferred_element_type=F32) + b_ref[...]
    for h in range(N_HEADS):
        ph = _rope(_head_norm(p[:, h * HEAD_DIM:(h + 1) * HEAD_DIM], qn), cos, sin)
        for j in range(tok // T_Q):
            q_ref[0, h, j] = ph[j * T_Q:(j + 1) * T_Q].T.astype(BF16)
    for h in range(N_KV_HEADS):
        ph = p[:, qw + h * HEAD_DIM: qw + (h + 1) * HEAD_DIM]
        k_ref[0, h] = _rope(_head_norm(ph, kn), cos, sin).astype(BF16)
        pv = p[:, qw + kw + h * HEAD_DIM: qw + kw + (h + 1) * HEAD_DIM]
        for j in range(tok // KV_SUB):
            v_ref[0, h, j] = _values_block(pv[j * KV_SUB:(j + 1) * KV_SUB])

    ph = jnp.dot(hh, w_ref[:, hy0:], preferred_element_type=F32) + b_ref[:, hy0:]
    top = jnp.where(i == 0, 0.0, ph[:HALO_F32])
    bot = jnp.where(i == nt - 1, 0.0, ph[HALO_F32:])
    pbuf[0:HALO_F32, :] = top
    pbuf[HALO_F32:HALO_F32 + tok, :] = p[:, hy0:]
    pbuf[HALO_F32 + tok:, :] = bot
    u = _dwconv3(pbuf[...], HALO_F32, tok, cw_ref[0:1, :], cw_ref[1:2, :], cw_ref[2:3, :], cb_ref[...])
    pad = jnp.zeros((T_PITCH - SLAB, d_hy3), F32)
    for j in range(tok // SLAB):
        u_ref[0, j * T_PITCH:j * T_PITCH + SLAB, :] = u[j * SLAB:(j + 1) * SLAB]
        u_ref[0, j * T_PITCH + SLAB:(j + 1) * T_PITCH, :] = pad


def _inproj(x, mod, n1w, w, b, qn, kn, cos_t, sin_t, cw, cb, ctx_len):
    bsz, seq, d = x.shape
    tok = T_TOK
    nt = seq // tok
    width = w.shape[1]
    qw = N_HEADS * HEAD_DIM
    kw = N_KV_HEADS * HEAD_DIM
    d_hy3 = width - qw - 2 * kw
    hb = tok // HALO_F32
    n_hblk = seq // HALO_F32
    kernel = functools.partial(_inproj_kernel, tok=tok, d_hy3=d_hy3)
    rows_p = (seq // SLAB) * T_PITCH
    out_shape = (
        jax.ShapeDtypeStruct((bsz, N_HEADS, seq // T_Q, HEAD_DIM, T_Q), BF16),
        jax.ShapeDtypeStruct((bsz, N_KV_HEADS, seq + ctx_len, HEAD_DIM), BF16),
        jax.ShapeDtypeStruct((bsz, N_KV_HEADS, (seq + ctx_len) // KV_SUB, V_ROWS, KV_SUB), BF16),
        jax.ShapeDtypeStruct((bsz, rows_p, d_hy3), F32),
    )
    return pl.pallas_call(
        kernel,
        out_shape=out_shape,
        grid=(bsz, nt),
        in_specs=[
            pl.BlockSpec((1, HALO_F32, d), lambda b_, i: (b_, jnp.maximum(i * hb - 1, 0), 0)),
            pl.BlockSpec((1, tok, d), lambda b_, i: (b_, i, 0)),
            pl.BlockSpec((1, HALO_F32, d), lambda b_, i: (b_, jnp.minimum((i + 1) * hb, n_hblk - 1), 0)),
            pl.BlockSpec((1, N_MOD, d), lambda b_, i: (b_, 0, 0)),
            _const_spec((1, d)),
            _const_spec((d, width)),
            _const_spec((1, width)),
            _const_spec((1, HEAD_DIM)),
            _const_spec((1, HEAD_DIM)),
            pl.BlockSpec((tok, HEAD_DIM), lambda b_, i: (i, 0)),
            pl.BlockSpec((tok, HEAD_DIM), lambda b_, i: (i, 0)),
            _const_spec((3, d_hy3)),
            _const_spec((1, d_hy3)),
            pl.BlockSpec(memory_space=pl.ANY), pl.BlockSpec(memory_space=pl.ANY),
        ],
        out_specs=(
            pl.BlockSpec((1, N_HEADS, tok // T_Q, HEAD_DIM, T_Q), lambda b_, i: (b_, 0, i, 0, 0)),
            pl.BlockSpec((1, N_KV_HEADS, tok, HEAD_DIM), lambda b_, i: (b_, 0, i, 0)),
            pl.BlockSpec((1, N_KV_HEADS, tok // KV_SUB, V_ROWS, KV_SUB), lambda b_, i: (b_, 0, i, 0, 0)),
            pl.BlockSpec((1, (tok // SLAB) * T_PITCH, d_hy3), lambda b_, i: (b_, i, 0)),
        ),
        scratch_shapes=[pltpu.VMEM((tok + 2 * HALO_F32, d_hy3), F32)],
        input_output_aliases={13: 1, 14: 2},
        compiler_params=pltpu.CompilerParams(
            dimension_semantics=("arbitrary", "arbitrary"), vmem_limit_bytes=VMEM_LIMIT),
        name="inproj",
    )(x, x, x, mod, n1w, w, b, qn, kn, cos_t, sin_t, cw, cb,
      jnp.zeros(out_shape[1].shape, BF16), jnp.zeros(out_shape[2].shape, BF16))


def _ctx_kv_kernel(c_ref, mod_ref, n1_ref, w_ref, b_ref, kn_ref, kin_ref, vin_ref, k_ref, v_ref):
    del kin_ref, vin_ref
    h = _mod_norm(c_ref[0], n1_ref[...], mod_ref[0, 0:1, :], mod_ref[0, 1:2, :]).astype(BF16)
    p = jnp.dot(h, w_ref[...], preferred_element_type=F32) + b_ref[...]
    kw = N_KV_HEADS * HEAD_DIM
    kn = kn_ref[...]
    for hd in range(N_KV_HEADS):
        k_ref[0, hd] = _head_norm(p[:, hd * HEAD_DIM:(hd + 1) * HEAD_DIM], kn).astype(BF16)
        pv = p[:, kw + hd * HEAD_DIM: kw + (hd + 1) * HEAD_DIM]
        for j in range(pv.shape[0] // KV_SUB):
            v_ref[0, hd, j] = _values_block(pv[j * KV_SUB:(j + 1) * KV_SUB])


def _ctx_kv(ctx, mod, n1w, w, b, kn, k_all, v_all, seq):
    bsz, ctx_len, d = ctx.shape
    ctx_row = bsz
    blk = seq // ctx_len
    any_spec = pl.BlockSpec(memory_space=pl.ANY)
    return pl.pallas_call(
        _ctx_kv_kernel,
        out_shape=(jax.ShapeDtypeStruct(k_all.shape, k_all.dtype),
                   jax.ShapeDtypeStruct(v_all.shape, v_all.dtype)),
        grid=(bsz,),
        in_specs=[
            pl.BlockSpec((1, ctx_len, d), lambda b_: (b_, 0, 0)),
            pl.BlockSpec((1, N_MOD, d), lambda b_: (ctx_row, 0, 0)),
            _const_spec((1, d)),
            _const_spec(w.shape),
            _const_spec(b.shape),
            _const_spec((1, HEAD_DIM)),
            any_spec, any_spec,
        ],
        out_specs=(
            pl.BlockSpec((1, N_KV_HEADS, ctx_len, HEAD_DIM), lambda b_: (b_, 0, blk, 0)),
            pl.BlockSpec((1, N_KV_HEADS, ctx_len // KV_SUB, V_ROWS, KV_SUB),
                         lambda b_: (b_, 0, seq // ctx_len, 0, 0)),
        ),
        input_output_aliases={6: 0, 7: 1},
        compiler_params=pltpu.CompilerParams(dimension_semantics=("arbitrary",)),
        name="ctx_kv",
    )(ctx, mod, n1w, w, b, kn, k_all, v_all)


def _attn_kernel(q_ref, k_ref, v_ref, o_ref, s_a, s_b, p_a, p_b, acc_ref, *, n_sub):
    tq = q_ref.shape[4]
    chains = [(g, j) for g in range(Q_GROUP) for j in range(ATTN_QSUBS)]
    assert n_sub % 2 == 1 and n_sub >= 3
    assert (n_sub - 1) % ATTN_STEPS == 0 and ATTN_STEPS % 2 == 0

    def one_pass(ps, carry):
        _attn_pass(ps * ATTN_QSUBS, chains, tq, n_sub, q_ref, k_ref, v_ref, o_ref, s_a, s_b, p_a, p_b, acc_ref)
        return carry

    lax.fori_loop(0, q_ref.shape[2] // ATTN_QSUBS, one_pass, 0)


def _attn_pass(q0, chains, tq, n_sub, q_ref, k_ref, v_ref, o_ref, s_a, s_b, p_a, p_b, acc_ref):
    def scores(c, s_out):
        start = c * KV_SUB
        if not isinstance(c, int):
            start = pl.multiple_of(start, KV_SUB)
        kc = k_ref[0, 0, pl.ds(start, KV_SUB), :]
        for ch, (g, j) in enumerate(chains):
            s_out[ch] = jnp.dot(kc, q_ref[0, g, q0 + j], preferred_element_type=F32)

    def softmax(s_in, p_out, stats):
        new_stats, alphas = [], []
        for ch in range(len(chains)):
            s = s_in[ch]
            m_new = jnp.maximum(stats[ch], jnp.max(s, axis=0, keepdims=True))
            alphas.append(jnp.exp2(stats[ch] - m_new))
            p_out[ch] = jnp.exp2(s - m_new).astype(BF16)
            new_stats.append(m_new)
        return tuple(new_stats), tuple(alphas)

    def values(c, p_in, alphas):
        vt = v_ref[0, 0, c]
        for ch in range(len(chains)):
            acc_ref[ch] = alphas[ch] * acc_ref[ch] + jnp.dot(vt, p_in[ch], preferred_element_type=F32)

    acc_ref[...] = jnp.zeros(acc_ref.shape, F32)
    stats = (jnp.full((1, tq), -jnp.inf, F32),) * len(chains)
    scores(0, s_a)
    scores(1, s_b)
    stats, alphas = softmax(s_a, p_a, stats)

    def group(t, carry):
        stats, alphas = carry
        for u in range(ATTN_STEPS):
            c = ATTN_STEPS * t + 1 + u
            s_next, s_cur, p_cur, p_prev = (s_a, s_b, p_b, p_a) if u % 2 == 0 else (s_b, s_a, p_a, p_b)
            scores(jnp.minimum(c + 1, n_sub - 1), s_next)
            stats, alphas_new = softmax(s_cur, p_cur, stats)
            values(c - 1, p_prev, alphas)
            alphas = alphas_new
        return stats, alphas

    stats, alphas = lax.fori_loop(0, (n_sub - 1) // ATTN_STEPS, group, (stats, alphas))
    values(n_sub - 1, p_a, alphas)
    for ch, (g, j) in enumerate(chains):
        out = acc_ref[ch, :HEAD_DIM, :] / acc_ref[ch, HEAD_DIM:HEAD_DIM + 1, :]
        row = pl.multiple_of((q0 + j) * tq, tq)
        o_ref[0, pl.ds(row, tq), g * HEAD_DIM:(g + 1) * HEAD_DIM] = out.T.astype(o_ref.dtype)


def _attention(q, k_all, v_all):
    bsz, _, n_qsub, _, tq = q.shape
    n_keys = k_all.shape[2]
    n_sub = v_all.shape[2]
    qs = ATTN_QSUBS * ATTN_PASSES
    n_chain = Q_GROUP * ATTN_QSUBS
    assert n_qsub % qs == 0
    kernel = functools.partial(_attn_kernel, n_sub=n_sub)
    return pl.pallas_call(
        kernel,
        out_shape=jax.ShapeDtypeStruct((bsz, n_qsub * tq, N_HEADS * HEAD_DIM), BF16),
        grid=(bsz, N_KV_HEADS, n_qsub // qs),
        in_specs=[
            pl.BlockSpec((1, Q_GROUP, qs, HEAD_DIM, tq), lambda b_, h, i: (b_, h, i, 0, 0)),
            pl.BlockSpec((1, 1, n_keys, HEAD_DIM), lambda b_, h, i: (b_, h, 0, 0)),
            pl.BlockSpec((1, 1, n_sub, V_ROWS, KV_SUB), lambda b_, h, i: (b_, h, 0, 0, 0)),
        ],
        out_specs=pl.BlockSpec((1, qs * tq, Q_GROUP * HEAD_DIM), lambda b_, h, i: (b_, i, h)),
        scratch_shapes=[
            pltpu.VMEM((n_chain, KV_SUB, tq), F32), pltpu.VMEM((n_chain, KV_SUB, tq), F32),
            pltpu.VMEM((n_chain, KV_SUB, tq), BF16), pltpu.VMEM((n_chain, KV_SUB, tq), BF16),
            pltpu.VMEM((n_chain, V_ROWS, tq), F32),
        ],
        compiler_params=pltpu.CompilerParams(
            dimension_semantics=("arbitrary", "arbitrary", "arbitrary"), vmem_limit_bytes=VMEM_LIMIT),
        name="attention",
    )(q, k_all, v_all)


def _filter_kernel(z_ref, tab_ref, dl_ref, w1_ref, b1_ref, wi_ref, bi_ref, fr_ref, wf_ref, wb_ref,
                   g_ref, *, rows, d_hy, n_inner):
    half = rows // 2
    fr = fr_ref[...]
    pre = (jnp.dot(z_ref[:half, :], w1_ref[0], precision=HIGHEST, preferred_element_type=F32)
           + jnp.dot(z_ref[half:, :], w1_ref[1], precision=HIGHEST, preferred_element_type=F32))
    h = jnp.sin(fr * (pre + b1_ref[...]))
    for j in range(n_inner):
        h = jnp.sin(fr * (jnp.dot(h, wi_ref[j], precision=HIGHEST,
                                  preferred_element_type=F32) + bi_ref[j]))
    hb = h.astype(BF16)
    pad = jnp.zeros((T_PITCH - SLAB, d_hy), F32)
    for part in range(2):
        pf = jnp.dot(hb, wf_ref[part], preferred_element_type=F32)
        pb = jnp.dot(hb, wb_ref[part], preferred_element_type=F32)
        r0 = part * half
        t = tab_ref[r0:r0 + half, 0:1]
        mf = tab_ref[r0:r0 + half, 1:2]
        mb = tab_ref[r0:r0 + half, 2:3]
        decay = jnp.exp(-t * dl_ref[...])
        for o in range(HYENA_ORDER):
            g = (mf * pf[:, o * d_hy:(o + 1) * d_hy] + mb * pb[:, o * d_hy:(o + 1) * d_hy]) * decay
            for j in range(half // SLAB):
                row = (r0 // SLAB + j) * T_PITCH
                g_ref[o, row:row + SLAB, :] = g[j * SLAB:(j + 1) * SLAB]
                g_ref[o, row + SLAB:row + T_PITCH, :] = pad


def _filter_time(z2, tab, absdelta, w1p, b1, wi, bi, freq, wf, wb):
    n_rows = z2.shape[0]
    d_hy = absdelta.shape[1]
    rows = T_FILT
    n_inner = wi.shape[0]
    kernel = functools.partial(_filter_kernel, rows=rows, d_hy=d_hy, n_inner=n_inner)
    return pl.pallas_call(
        kernel,
        out_shape=jax.ShapeDtypeStruct((HYENA_ORDER, (n_rows // SLAB) * T_PITCH, d_hy), F32),
        grid=(n_rows // rows,),
        in_specs=[
            pl.BlockSpec((rows, z2.shape[1]), lambda i: (i, 0)),
            pl.BlockSpec((rows, tab.shape[1]), lambda i: (i, 0)),
            _const_spec(absdelta.shape), _const_spec(w1p.shape), _const_spec(b1.shape),
            _const_spec(wi.shape), _const_spec(bi.shape), _const_spec(freq.shape),
            _const_spec(wf.shape), _const_spec(wb.shape),
        ],
        out_specs=pl.BlockSpec((HYENA_ORDER, (rows // SLAB) * T_PITCH, d_hy), lambda i: (0, i, 0)),
        compiler_params=pltpu.CompilerParams(
            dimension_semantics=("arbitrary",), vmem_limit_bytes=VMEM_LIMIT),
        name="hyena_filter",
    )(z2, tab, absdelta, w1p, b1, wi, bi, freq, wf, wb)


def _loop(n, body, unroll):
    def group(t, carry):
        for j in range(unroll):
            body(t * unroll + j)
        return carry

    lax.fori_loop(0, n // unroll, group, 0)
    for i in range(n - n % unroll, n):
        body(i)


def _slab_start(p):
    start = p * S_PITCH
    return start if isinstance(p, int) else pl.multiple_of(start, SUBLANES)


def _forward_cross_slab(src_ref, n_in, ff_ref, s_ref, half):
    def body(r):
        g = src_ref[pl.ds(r, n_in, stride=T_PITCH), :]
        a = jnp.dot(ff_ref[r], g.astype(BF16), preferred_element_type=F32)
        s_ref[pl.ds(r, half, stride=S_PITCH), :] = a[:half]
        s_ref[pl.ds(SLAB + r, half, stride=S_PITCH), :] = a[half:]

    _loop(SLAB, body, CROSS_UNROLL)


def _filter_spec_kernel(g_ref, ff_ref, fa_ref, k_ref, s_ref, *, n_in, n_p, half):
    _forward_cross_slab(g_ref.at[0], n_in, ff_ref, s_ref, half)
    fa = fa_ref[...]

    def slabs(p0, count):
        a = jnp.concatenate([s_ref[pl.ds(_slab_start(p0 + j), 2 * SLAB), :] for j in range(count)], axis=1)
        x = jnp.dot(fa, a.astype(BF16), preferred_element_type=F32).astype(k_ref.dtype)
        for j in range(count):
            k_ref[0, p0 + j] = x[:, j * LANES:(j + 1) * LANES]

    _loop(n_p // SLAB_GROUP, lambda i: slabs(i * SLAB_GROUP, SLAB_GROUP), SLAB_UNROLL // SLAB_GROUP)
    if n_p % SLAB_GROUP:
        slabs(n_p - n_p % SLAB_GROUP, n_p % SLAB_GROUP)


def _filter_spectrum(g, ff_full, fa, n_p):
    n_ord, rows_p, d_hy = g.shape
    n_in = rows_p // T_PITCH
    half = ff_full.shape[1] // 2
    kernel = functools.partial(_filter_spec_kernel, n_in=n_in, n_p=n_p, half=half)
    return pl.pallas_call(
        kernel,
        out_shape=jax.ShapeDtypeStruct((n_ord, n_p, 2 * SLAB, d_hy), BF16),
        grid=(n_ord, d_hy // LANES),
        in_specs=[
            pl.BlockSpec((1, rows_p, LANES), lambda o, c: (o, 0, c)),
            _const_spec(ff_full.shape),
            _const_spec(fa.shape),
        ],
        out_specs=pl.BlockSpec((1, n_p, 2 * SLAB, LANES), lambda o, c: (o, 0, 0, c)),
        scratch_shapes=[pltpu.VMEM((half * S_PITCH, LANES), F32)],
        compiler_params=pltpu.CompilerParams(
            dimension_semantics=("arbitrary", "arbitrary"), vmem_limit_bytes=VMEM_LIMIT),
        name="hyena_filter_spectrum",
    )(g, ff_full, fa)


def _conv_kernel(z_ref, x_ref, k_ref, sk_ref, ff_ref, fa_ref, fai_ref, gi_ref, o_ref, s_ref,
                 *, n_a, n_p, half):
    _forward_cross_slab(z_ref.at[0], n_a, ff_ref, s_ref, half)
    fa = fa_ref[...]
    fai = fai_ref[...]

    def slabs(p0, count):
        starts = [_slab_start(p0 + j) for j in range(count)]
        a = jnp.concatenate([s_ref[pl.ds(st, 2 * SLAB), :] for st in starts], axis=1)
        x = jnp.dot(fa, a.astype(BF16), preferred_element_type=F32)
        ys = []
        for j in range(count):
            kk = k_ref[0, p0 + j].astype(F32)
            xr, xi = x[:SLAB, j * LANES:(j + 1) * LANES], x[SLAB:, j * LANES:(j + 1) * LANES]
            kr, ki = kk[:SLAB], kk[SLAB:]
            ys.append(jnp.concatenate([xr * kr - xi * ki, xr * ki + xi * kr], axis=0))
        b = jnp.dot(fai, jnp.concatenate(ys, axis=1).astype(BF16), preferred_element_type=F32)
        for j in range(count):
            s_ref[pl.ds(starts[j], 2 * SLAB), :] = b[:, j * LANES:(j + 1) * LANES]

    _loop(n_p // SLAB_GROUP, lambda i: slabs(i * SLAB_GROUP, SLAB_GROUP), SLAB_UNROLL // SLAB_GROUP)
    if n_p % SLAB_GROUP:
        slabs(n_p - n_p % SLAB_GROUP, n_p % SLAB_GROUP)
    skip = sk_ref[0]

    def inverse(r):
        br = s_ref[pl.ds(r, half, stride=S_PITCH), :]
        bi = s_ref[pl.ds(SLAB + r, half, stride=S_PITCH), :]
        bb = jnp.concatenate([br, bi], axis=0).astype(BF16)
        y = jnp.dot(gi_ref[r], bb, preferred_element_type=F32)
        z = z_ref[0, pl.ds(r, n_a, stride=T_PITCH), :]
        gate = x_ref[0, pl.ds(r, n_a, stride=T_PITCH), :]
        o_ref[0, pl.ds(r, n_a, stride=T_PITCH), :] = gate * (y + skip * z)

    _loop(SLAB, inverse, CROSS_UNROLL)
    pad = jnp.zeros((T_PITCH - SLAB, LANES), F32)
    for a in range(n_a):
        o_ref[0, a * T_PITCH + SLAB:(a + 1) * T_PITCH, :] = pad


def _hyena_conv(zin, z_blk0, gate, g_blk0, kspec, order, skip, ff, fa, fai, gi, d_hy):
    bsz, rows_p, _ = zin.shape
    n_a = rows_p // T_PITCH
    n_p = kspec.shape[1]
    half = ff.shape[1] // 2
    kernel = functools.partial(_conv_kernel, n_a=n_a, n_p=n_p, half=half)
    return pl.pallas_call(
        kernel,
        out_shape=jax.ShapeDtypeStruct((bsz, rows_p, d_hy), F32),
        grid=(d_hy // LANES, bsz),
        in_specs=[
            pl.BlockSpec((1, rows_p, LANES), lambda c, b_: (b_, 0, z_blk0 + c)),
            pl.BlockSpec((1, rows_p, LANES), lambda c, b_: (b_, 0, g_blk0 + c)),
            pl.BlockSpec((1, n_p, 2 * SLAB, LANES), lambda c, b_: (order, 0, 0, c),
                         pipeline_mode=pl.Buffered(1)),
            pl.BlockSpec((1, 1, LANES), lambda c, b_: (order, 0, c)),
            _const_spec(ff.shape), _const_spec(fa.shape), _const_spec(fai.shape),
            _const_spec(gi.shape),
        ],
        out_specs=pl.BlockSpec((1, rows_p, LANES), lambda c, b_: (b_, 0, c)),
        scratch_shapes=[pltpu.VMEM((half * S_PITCH, LANES), F32)],
        compiler_params=pltpu.CompilerParams(
            dimension_semantics=("arbitrary", "arbitrary"), vmem_limit_bytes=VMEM_LIMIT),
        name=f"hyena_conv{order}",
    )(zin, gate, kspec, skip, ff, fa, fai, gi)


def _merge_kernel(x_ref, mod_ref, n1_ref, n2_ref, wg_ref, bg_ref, a_ref, hy_ref, wa_ref, wh_ref,
                  wo_ref, bo_ref, xo_ref, h2_ref, *, tok, d):
    x = x_ref[0]
    sh1, sc1, g1 = mod_ref[0, 0:1, :], mod_ref[0, 1:2, :], mod_ref[0, 2:3, :]
    sh2, sc2 = mod_ref[0, 3:4, :], mod_ref[0, 4:5, :]
    h1 = _mod_norm(x, n1_ref[...], sh1, sc1).astype(BF16)
    gates = jnp.dot(h1, wg_ref[...], preferred_element_type=F32) + bg_ref[...]
    hy = jnp.concatenate([hy_ref[0, j * T_PITCH:j * T_PITCH + SLAB, :] for j in range(tok // SLAB)],
                         axis=0)
    pa = jnp.dot(a_ref[0], wa_ref[...], preferred_element_type=F32)
    ph = jnp.dot(hy.astype(BF16), wh_ref[...], preferred_element_type=F32)
    mixed = jax.nn.sigmoid(gates[:, :d]) * pa + jax.nn.sigmoid(gates[:, d:]) * ph
    y = jnp.dot(mixed.astype(BF16), wo_ref[...], preferred_element_type=F32) + bo_ref[...]
    xn = x + g1 * y
    xo_ref[0] = xn
    h2_ref[0] = _mod_norm(xn, n2_ref[...], sh2, sc2).astype(BF16)


def _merge(x, mod, n1w, n2w, wg, bg, attn_o, hy_o, wa, wh, wo, bo):
    bsz, seq, d = x.shape
    tok = T_TOK
    d_hy = hy_o.shape[2]
    kernel = functools.partial(_merge_kernel, tok=tok, d=d)
    return pl.pallas_call(
        kernel,
        out_shape=(jax.ShapeDtypeStruct((bsz, seq, d), F32), jax.ShapeDtypeStruct((bsz, seq, d), BF16)),
        grid=(bsz, seq // tok),
        in_specs=[
            pl.BlockSpec((1, tok, d), lambda b_, i: (b_, i, 0)),
            pl.BlockSpec((1, N_MOD, d), lambda b_, i: (b_, 0, 0)),
            _const_spec((1, d)), _const_spec((1, d)),
            _const_spec(wg.shape), _const_spec(bg.shape),
            pl.BlockSpec((1, tok, attn_o.shape[2]), lambda b_, i: (b_, i, 0)),
            pl.BlockSpec((1, (tok // SLAB) * T_PITCH, d_hy), lambda b_, i: (b_, i, 0)),
            _const_spec(wa.shape), _const_spec(wh.shape), _const_spec(wo.shape), _const_spec(bo.shape),
        ],
        out_specs=(pl.BlockSpec((1, tok, d), lambda b_, i: (b_, i, 0)),
                   pl.BlockSpec((1, tok, d), lambda b_, i: (b_, i, 0))),
        compiler_params=pltpu.CompilerParams(
            dimension_semantics=("arbitrary", "arbitrary"), vmem_limit_bytes=VMEM_LIMIT),
        name="merge",
    )(x, mod, n1w, n2w, wg, bg, attn_o, hy_o, wa, wh, wo, bo)


def _gelu_tanh(x):
    return 0.5 * x * (1.0 + jnp.tanh(math.sqrt(2.0 / math.pi) * (x + 0.044715 * (x * x * x))))


def _ffn_kernel(hp_ref, hm_ref, hn_ref, x_ref, mod_ref, wu_ref, bu_ref, cw_ref, cb_ref, wd_ref, bd_ref,
                fn_ref, o_ref, pa_buf, pg_buf, act_buf, *, tok, d_ff):
    i = pl.program_id(1)
    nt = pl.num_programs(1)
    hext = jnp.concatenate([hp_ref[0], hm_ref[0], hn_ref[0]], axis=0)
    lo = HALO_BF16

    def fill(buf, col0):
        cs = slice(col0, col0 + FF_CHUNK)
        p = jnp.dot(hext, wu_ref[:, cs], preferred_element_type=F32)
        nb = -bu_ref[:, cs]
        buf[0:lo, :] = jnp.where(i == 0, nb, p[0:lo])
        buf[lo:lo + tok, :] = p[lo:lo + tok]
        buf[lo + tok:, :] = jnp.where(i == nt - 1, nb, p[lo + tok:])

    def conv(buf, col0):
        cs = slice(col0, col0 + FF_CHUNK)
        w0, w1, w2 = cw_ref[0:1, cs], cw_ref[1:2, cs], cw_ref[2:3, cs]
        const = (w0 + w1 + w2) * bu_ref[:, cs] + cb_ref[:, cs]
        return _dwconv3(buf[...], lo, tok, w0, w1, w2, const)

    for c in range(d_ff // FF_CHUNK):
        ca = c * FF_CHUNK
        cg = d_ff + c * FF_CHUNK
        fill(pa_buf, ca)
        fill(pg_buf, cg)
        act_buf[:, ca:ca + FF_CHUNK] = (_gelu_tanh(conv(pa_buf, ca)) * conv(pg_buf, cg)).astype(BF16)
    y = jnp.dot(act_buf[...], wd_ref[...], preferred_element_type=F32)
    g2 = mod_ref[0, 5:6, :]
    xn = x_ref[0] + g2 * (y + bd_ref[...])
    o_ref[0] = _rms(xn) * fn_ref[...]


def _ffn(h2, x_new, mod, wu, bu, cw, cb, wd, bd, fnw):
    bsz, seq, d = x_new.shape
    tok = T_FFN
    d_ff = wd.shape[0]
    hb = tok // HALO_BF16
    n_hblk = seq // HALO_BF16
    rows = tok + 2 * HALO_BF16
    kernel = functools.partial(_ffn_kernel, tok=tok, d_ff=d_ff)
    return pl.pallas_call(
        kernel,
        out_shape=jax.ShapeDtypeStruct((bsz, seq, d), F32),
        grid=(bsz, seq // tok),
        in_specs=[
            pl.BlockSpec((1, HALO_BF16, d), lambda b_, i: (b_, jnp.maximum(i * hb - 1, 0), 0)),
            pl.BlockSpec((1, tok, d), lambda b_, i: (b_, i, 0)),
            pl.BlockSpec((1, HALO_BF16, d), lambda b_, i: (b_, jnp.minimum((i + 1) * hb, n_hblk - 1), 0)),
            pl.BlockSpec((1, tok, d), lambda b_, i: (b_, i, 0)),
            pl.BlockSpec((1, N_MOD, d), lambda b_, i: (b_, 0, 0)),
            _const_spec(wu.shape), _const_spec(bu.shape), _const_spec(cw.shape), _const_spec(cb.shape),
            _const_spec(wd.shape), _const_spec(bd.shape), _const_spec(fnw.shape),
        ],
        out_specs=pl.BlockSpec((1, tok, d), lambda b_, i: (b_, i, 0)),
        scratch_shapes=[pltpu.VMEM((rows, FF_CHUNK), F32), pltpu.VMEM((rows, FF_CHUNK), F32),
                        pltpu.VMEM((tok, d_ff), BF16)],
        compiler_params=pltpu.CompilerParams(
            dimension_semantics=("arbitrary", "arbitrary"), vmem_limit_bytes=VMEM_LIMIT),
        name="conv_ffn",
    )(h2, h2, h2, x_new, mod, wu, bu, cw, cb, wd, bd, fnw)


def _rope_tables(seq):
    half = HEAD_DIM // 2
    t = np.arange(seq)
    row = (t // GRID_W).astype(np.float64)
    col = (t % GRID_W).astype(np.float64)
    freqs = ROPE_THETA ** (-np.arange(0, half, 2, dtype=np.float64) / half)
    ang = np.concatenate([row[:, None] * freqs, col[:, None] * freqs], axis=-1)
    cos = np.cos(ang)
    sin = np.sin(ang)
    cos_t = np.concatenate([cos, cos], axis=-1)
    sin_t = np.concatenate([-sin, sin], axis=-1)
    return jnp.asarray(cos_t, F32), jnp.asarray(sin_t, F32)


def _filter_tables(seq, d_hy):
    n = 2 * seq
    j = np.arange(n)
    lag = np.where(j < seq, j, n - j)
    t = lag / max(seq - 1, 1)
    bands = (FILTER_EMB_DIM - 1) // 2
    f = np.linspace(1e-4, bands - 1, bands)
    wpos = 2.0 * math.pi * lag / seq
    z = np.concatenate([t[:, None], np.cos(wpos[:, None] * f), -np.sin(wpos[:, None] * f)], axis=-1)
    z2 = np.zeros((n, LANES), np.float64)
    z2[:, :FILTER_EMB_DIM] = z
    tab = np.zeros((n, SUBLANES), np.float64)
    tab[:, 0] = t
    tab[:, 1] = (j < seq)
    tab[:, 2] = np.logical_or(j == 0, j > seq)
    min_decay = math.log(DECAY_TARGET) / SLOW_DECAY_PCT
    max_decay = math.log(DECAY_TARGET) / FAST_DECAY_PCT
    absdelta = np.abs(np.linspace(min_decay, max_decay, d_hy))[None, :]
    return jnp.asarray(z2, F32), jnp.asarray(tab, F32), jnp.asarray(absdelta, F32)


def _dft_tables(seq):
    n_a = seq // SLAB
    n_s = 2 * n_a
    n = 2 * seq
    n_p = n_a + 1
    half = _round_up(n_p, SUBLANES)
    p = np.arange(half)
    live = (p < n_p).astype(np.float64)
    r = np.arange(SLAB)
    a = np.arange(n_s)
    theta = 2.0 * math.pi * p[None, :, None] * (SLAB * a[None, None, :] + r[:, None, None]) / n
    ff = np.concatenate([np.cos(theta) * live[None, :, None], -np.sin(theta) * live[None, :, None]], axis=1)
    phi = 2.0 * math.pi * np.outer(r, r) / SLAB
    c, s = np.cos(phi), np.sin(phi)
    fa = np.block([[c, s], [-s, c]])
    fai = np.block([[c, -s], [s, c]])
    wgt = np.where((p == 0) | (p == n_a), 1.0, 2.0) * live / n
    th_i = 2.0 * math.pi * (SLAB * a[None, :n_a, None] + r[:, None, None]) * p[None, None, :] / n
    gi = np.concatenate([np.cos(th_i) * wgt[None, None, :], -np.sin(th_i) * wgt[None, None, :]], axis=2)
    to_bf = lambda m: jnp.asarray(m, F32).astype(BF16)
    return to_bf(ff), to_bf(ff[:, :, :n_a]), to_bf(fa), to_bf(fai), to_bf(gi), n_p


def kernel(x, c, ctx, c_ctx, w_mod, b_mod, norm1_w, norm2_w, w_in, b_in, q_norm_w, k_norm_w, hy_conv_w, hy_conv_b, filt_w1, filt_b1, filt_w_inner, filt_b_inner, filt_freq, filt_w_out, hy_skip, w_attn_out, w_hy_out, w_o, b_o, w_up, b_up, ffn_conv_w, ffn_conv_b, w_down, b_down, final_norm_w):
    bsz, seq, d = x.shape
    ctx_len = ctx.shape[1]
    d_hy = hy_skip.shape[2]
    assert w_mod.shape[0] == 1, "single-layer kernel"
    assert seq % T_TOK == 0 and seq % T_Q == 0 and seq % ctx_len == 0 and (2 * seq) % T_FILT == 0
    assert T_TOK % KV_SUB == 0 and ctx_len % KV_SUB == 0 and seq % T_FFN == 0
    assert d_hy % LANES == 0 and w_down.shape[1] % FF_CHUNK == 0 and bsz + 1 <= SUBLANES

    cvec = jnp.zeros((SUBLANES, d), F32).at[:bsz].set(c).at[bsz].set(c_ctx)
    mod = _modulation(cvec, w_mod[0], b_mod[0]).reshape(SUBLANES, N_MOD, d)

    qw = N_HEADS * HEAD_DIM
    kw = N_KV_HEADS * HEAD_DIM
    hy0 = qw + 2 * kw
    g0 = hy0 + (HYENA_ORDER + 1) * d_hy
    def deinterleave(a):
        lead = a.shape[:-1]
        a = a.reshape(lead + (-1, HEAD_DIM // 2, 2))
        return jnp.swapaxes(a, -1, -2).reshape(lead + (-1,))

    w_f32, b_f32 = w_in[0], b_in[0].reshape(1, -1)
    w_main = jnp.concatenate([deinterleave(w_f32[:, :qw + kw]), w_f32[:, qw + kw:g0]], axis=1).astype(BF16)
    b_main = jnp.concatenate([deinterleave(b_f32[:, :qw + kw]), b_f32[:, qw + kw:g0]], axis=1)
    w_kv = w_main[:, qw:hy0]
    b_kv = b_main[:, qw:hy0]
    w_gates = w_f32[:, g0:].astype(BF16)
    b_gates = b_f32[:, g0:]
    qn = deinterleave(q_norm_w[0].reshape(1, HEAD_DIM)) * (math.log2(math.e) * HEAD_DIM ** -0.5)
    kn = deinterleave(k_norm_w[0].reshape(1, HEAD_DIM))
    cos_t, sin_t = _rope_tables(seq)
    n1w = norm1_w[0].reshape(1, d)
    n2w = norm2_w[0].reshape(1, d)

    q, k_all, v_all, u = _inproj(x, mod, n1w, w_main, b_main, qn, kn, cos_t, sin_t,
                                 hy_conv_w[0], hy_conv_b[0].reshape(1, -1), ctx_len)
    k_all, v_all = _ctx_kv(ctx, mod, n1w, w_kv, b_kv, kn, k_all, v_all, seq)
    attn_o = _attention(q, k_all, v_all)

    z2, tab, absdelta = _filter_tables(seq, d_hy)
    hid = filt_w1.shape[2]
    assert 2 * hid == LANES

    def lane_block(w, s):
        z = jnp.zeros_like(w)
        return jnp.concatenate([w, z] if s == 0 else [z, w], axis=1)

    def row_block(w, s):
        z = jnp.zeros_like(w)
        return jnp.concatenate([w, z] if s == 0 else [z, w], axis=0)

    w1p = jnp.zeros((LANES, hid), F32).at[:FILTER_EMB_DIM].set(filt_w1[0])
    w1_2 = jnp.stack([lane_block(w1p, 0), lane_block(w1p, 1)])
    wi_2 = jnp.stack([jnp.concatenate([lane_block(w, 0), lane_block(w, 1)], axis=0)
                      for w in filt_w_inner[0]])
    twice = lambda v: jnp.concatenate([v, v], axis=-1)
    w_out = filt_w_out[0].reshape(-1, 2, HYENA_ORDER * d_hy).astype(BF16)
    wf_2 = jnp.stack([row_block(w_out[:, 0], 0), row_block(w_out[:, 0], 1)])
    wb_2 = jnp.stack([row_block(w_out[:, 1], 0), row_block(w_out[:, 1], 1)])
    g_time = _filter_time(z2, tab, absdelta, w1_2, twice(filt_b1[0].reshape(1, -1)), wi_2,
                          twice(filt_b_inner[0][:, None, :]), twice(filt_freq[0].reshape(1, -1)),
                          wf_2, wb_2)
    ff_full, ff_half, fa, fai, gi, n_p = _dft_tables(seq)
    kspec = _filter_spectrum(g_time, ff_full, fa, n_p)
    skip = hy_skip[0].reshape(HYENA_ORDER, 1, d_hy)
    lanes_per_stream = d_hy // LANES
    z1 = _hyena_conv(u, 2 * lanes_per_stream, u, 0, kspec, 0, skip, ff_half, fa, fai, gi, d_hy)
    hy_o = _hyena_conv(z1, 0, u, lanes_per_stream, kspec, 1, skip, ff_half, fa, fai, gi, d_hy)

    x_new, h2 = _merge(x, mod, n1w, n2w, w_gates, b_gates, attn_o, hy_o,
                       w_attn_out[0].astype(BF16), w_hy_out[0].astype(BF16), w_o[0].astype(BF16),
                       b_o[0].reshape(1, d))
    return _ffn(h2, x_new, mod, w_up[0].astype(BF16), b_up[0].reshape(1, -1), ffn_conv_w[0],
                ffn_conv_b[0].reshape(1, -1), w_down[0].astype(BF16), b_down[0].reshape(1, d),
                final_norm_w.reshape(1, d))
```
